```python
import math
import jax, jax.numpy as jnp
from jax import lax
import numpy as np

D_MODEL = 2048
BATCH = 16
SEQ = 2048
DEPTH = 4

BRANCH_W = D_MODEL // 2
N_BRANCHES = 3
CONV_A_WIDTH = 31
ATT_PATTERNS = ((128, 1), (512, 4), (2048, 16))
ATT_GROUPS = len(ATT_PATTERNS)
ATT_HEADS = 8
ATT_HEAD_DIM = BRANCH_W // ATT_HEADS
ATT_W = ATT_GROUPS * ATT_HEADS * ATT_HEAD_DIM
ATT_BLOCK = 128
CONV_C_WIDTH = 3
OFF_Q = 2 * BRANCH_W
OFF_K = OFF_Q + ATT_W
OFF_V = OFF_K + ATT_W
OFF_C = OFF_V + ATT_W
OFF_G = OFF_C + 3 * BRANCH_W
N_IN = OFF_G + N_BRANCHES * D_MODEL
MOE_EXPERTS = 64
MOE_TOPK = 8
MOE_GROUPS = 8
MOE_TOPK_GROUPS = 4
MOE_FFN = D_MODEL // 8
MOE_SHARED_FFN = MOE_FFN
MOE_ROUTED_SCALE = 2.5
MOE_BLOCK = 128
LN_EPS = 1e-5
NEG_INF = -1e30

kernel_name = "hybrid_gated_conformer_dilated_shortconv_moe"


def layer_norm(x, g, b):
    xf = x.astype(jnp.float32)
    mu = jnp.mean(xf, axis=-1, keepdims=True)
    xc = xf - mu
    var = jnp.mean(xc * xc, axis=-1, keepdims=True)
    return (xc * lax.rsqrt(var + LN_EPS) * g + b).astype(x.dtype)


def causal_depthwise_conv(x, w):
    taps = w.shape[0]
    return lax.conv_general_dilated(
        x, w[:, None, :].astype(x.dtype), window_strides=(1,), padding=[(taps - 1, 0)],
        dimension_numbers=("NWC", "WIO", "NWC"), feature_group_count=x.shape[-1])


def dilated_window_attention(q, k, v, dilation, span):
    B, S, H, dh = q.shape
    L = S // dilation
    nb = -(-L // ATT_BLOCK)
    Lp = nb * ATT_BLOCK

    def classes(t):
        t = t.reshape(B, L, dilation, H, dh).transpose(0, 2, 3, 1, 4)
        return jnp.pad(t, ((0, 0), (0, 0), (0, 0), (0, Lp - L), (0, 0)))

    def band(t):
        t = jnp.pad(t, ((0, 0), (0, 0), (0, 0), (ATT_BLOCK, 0), (0, 0)))
        t = t.reshape(B, dilation, H, nb + 1, ATT_BLOCK, dh)
        return jnp.concatenate([t[:, :, :, :-1], t[:, :, :, 1:]], axis=-2)

    qb = classes(q).reshape(B, dilation, H, nb, ATT_BLOCK, dh)
    kb = band(classes(k))
    vb = band(classes(v))
    qi = jnp.arange(ATT_BLOCK)[:, None]
    ki = jnp.arange(2 * ATT_BLOCK)[None, :]
    dist = ATT_BLOCK + qi - ki
    key_pos = (jnp.arange(nb)[:, None, None] - 1) * ATT_BLOCK + ki[None]
    valid = (dist >= 0) & (dist <= span) & (key_pos >= 0)
    s = jnp.einsum("brhnqd,brhnkd->brhnqk", qb, kb,
                   preferred_element_type=jnp.float32) * (dh ** -0.5)
    s = jnp.where(valid, s, NEG_INF)
    lse = jax.nn.logsumexp(s, axis=-1)
    p = jnp.exp(s - lse[..., None]).astype(v.dtype)
    o = jnp.einsum("brhnqk,brhnkd->brhnqd", p, vb)
    o = o.reshape(B, dilation, H, Lp, dh)[:, :, :, :L].transpose(0, 3, 1, 2, 4).reshape(B, S, H, dh)
    lse = lse.reshape(B, dilation, H, Lp)[..., :L].transpose(0, 3, 1, 2).reshape(B, S, H)
    return o, lse


def token_mixer(x, w_in, conv_a_w, conv_a_b, ln_a_g, ln_a_b, conv_c_w, w_branch, w_out):
    B, S, _ = x.shape
    u = x @ w_in
    a_in, q, k, v, c_in, g_in = jnp.split(u, [OFF_Q, OFF_K, OFF_V, OFF_C, OFF_G], axis=-1)

    a_val, a_gate = jnp.split(a_in, 2, axis=-1)
    a = a_val * jax.nn.sigmoid(a_gate)
    a = causal_depthwise_conv(a, conv_a_w) + conv_a_b
    a = jax.nn.silu(layer_norm(a, ln_a_g, ln_a_b))

    shp = (B, S, ATT_GROUPS, ATT_HEADS, ATT_HEAD_DIM)
    q, k, v = q.reshape(shp), k.reshape(shp), v.reshape(shp)
    outs, lses = [], []
    for g, (window, dilation) in enumerate(ATT_PATTERNS):
        o, lse = dilated_window_attention(q[:, :, g], k[:, :, g], v[:, :, g], dilation, window // dilation)
        outs.append(o)
        lses.append(lse)
    wgt = jax.nn.softmax(jnp.stack(lses, axis=0), axis=0).astype(x.dtype)
    att = jnp.einsum("gbsh,gbshd->bshd", wgt, jnp.stack(outs, axis=0)).reshape(B, S, BRANCH_W)

    cb, cc, ch = jnp.split(c_in, 3, axis=-1)
    c = cb * causal_depthwise_conv(cc * ch, conv_c_w)

    proj = jnp.einsum("nbsc,ncd->nbsd", jnp.stack([a, att, c], axis=0), w_branch)
    gates = jax.nn.sigmoid(g_in).reshape(B, S, N_BRANCHES, D_MODEL)
    merged = jnp.einsum("bsnd,nbsd->bsd", gates, proj)
    return merged @ w_out


def route(h, router_w, router_bias):
    T = h.shape[0]
    logits = jnp.dot(h, router_w, preferred_element_type=jnp.float32)
    scores = jax.nn.sigmoid(logits)
    choice = scores + router_bias.astype(jnp.float32)
    grp = choice.reshape(T, MOE_GROUPS, MOE_EXPERTS // MOE_GROUPS)
    grp_score = lax.top_k(grp, 2)[0].sum(-1)
    _, top_grp = lax.top_k(grp_score, MOE_TOPK_GROUPS)
    grp_mask = (top_grp[..., None] == jnp.arange(MOE_GROUPS)).any(axis=-2)
    expert_mask = jnp.repeat(grp_mask, MOE_EXPERTS // MOE_GROUPS, axis=-1)
    _, idx = lax.top_k(jnp.where(expert_mask, choice, NEG_INF), MOE_TOPK)
    w = jnp.take_along_axis(scores, idx, axis=-1)
    w = w / jnp.sum(w, axis=-1, keepdims=True) * MOE_ROUTED_SCALE
    return idx, w


def moe(h, router_w, router_bias, w_gate, w_up, w_down, ws_gate, ws_up, ws_down):
    T, D = h.shape
    idx, wts = route(h, router_w, router_bias)
    TK = T * MOE_TOPK
    flat_e = idx.reshape(-1).astype(jnp.int32)
    flat_t = jnp.broadcast_to(jnp.arange(T, dtype=jnp.int32)[:, None], (T, MOE_TOPK)).reshape(-1)
    flat_w = wts.reshape(-1)
    order = jnp.argsort(flat_e)
    e_s, t_s, w_s = flat_e[order], flat_t[order], flat_w[order]
    counts = jnp.bincount(flat_e, length=MOE_EXPERTS).astype(jnp.int32)
    padded = (counts + MOE_BLOCK - 1) // MOE_BLOCK * MOE_BLOCK
    start = jnp.cumsum(counts) - counts
    pend = jnp.cumsum(padded)
    pstart = pend - padded
    dest = pstart[e_s] + (jnp.arange(TK, dtype=jnp.int32) - start[e_s])
    n_pad = -(-TK // MOE_BLOCK) * MOE_BLOCK + MOE_EXPERTS * MOE_BLOCK
    nb = n_pad // MOE_BLOCK
    row_tok = jnp.full((n_pad,), T, jnp.int32).at[dest].set(t_s)
    row_w = jnp.zeros((n_pad,), h.dtype).at[dest].set(w_s.astype(h.dtype))
    blk_expert = jnp.clip(jnp.searchsorted(pend, jnp.arange(nb, dtype=jnp.int32) * MOE_BLOCK, side="right"),
                          0, MOE_EXPERTS - 1)
    hp = jnp.concatenate([h, jnp.zeros((1, D), h.dtype)], axis=0)

    def step(acc, blk):
        tok, w, e = blk
        xb = hp[tok]
        y = (jax.nn.silu(xb @ w_gate[e]) * (xb @ w_up[e])) @ w_down[e]
        return acc.at[tok].add(y * w[:, None]), None

    acc, _ = lax.scan(step, jnp.zeros((T + 1, D), h.dtype),
                      (row_tok.reshape(nb, MOE_BLOCK), row_w.reshape(nb, MOE_BLOCK), blk_expert))
    shared = (jax.nn.silu(h @ ws_gate) * (h @ ws_up)) @ ws_down
    return acc[:T] + shared


def setup_inputs(seed: int = 0) -> dict:
    key = jax.random.key(seed)
    ks = jax.random.split(key, 21)
    L, D, F = DEPTH, D_MODEL, MOE_FFN
    beta = (8.0 * DEPTH) ** -0.25

    def nrm(k, shape, scale):
        return jax.random.normal(k, shape, jnp.float32) * scale

    return {
        "x": nrm(ks[0], (BATCH, SEQ, D), 1.0),
        "w_in": nrm(ks[1], (L, D, N_IN), D ** -0.5),
        "conv_a_w": nrm(ks[2], (L, CONV_A_WIDTH, BRANCH_W), CONV_A_WIDTH ** -0.5),
        "conv_a_b": nrm(ks[3], (L, BRANCH_W), 0.01),
        "ln_a_g": 1.0 + nrm(ks[4], (L, BRANCH_W), 0.01),
        "ln_a_b": nrm(ks[5], (L, BRANCH_W), 0.01),
        "conv_c_w": nrm(ks[6], (L, CONV_C_WIDTH, BRANCH_W), CONV_C_WIDTH ** -0.5),
        "w_branch": nrm(ks[7], (L, N_BRANCHES, BRANCH_W, D), BRANCH_W ** -0.5),
        "w_out": nrm(ks[8], (L, D, D), D ** -0.5 * beta),
        "ln1_g": 1.0 + nrm(ks[9], (L, D), 0.01),
        "ln1_b": nrm(ks[10], (L, D), 0.01),
        "router_w": nrm(ks[11], (L, D, MOE_EXPERTS), D ** -0.5),
        "router_bias": nrm(ks[12], (L, MOE_EXPERTS), 0.01),
        "w_gate": nrm(ks[13], (L, MOE_EXPERTS, D, F), D ** -0.5),
        "w_up": nrm(ks[14], (L, MOE_EXPERTS, D, F), D ** -0.5),
        "w_down": nrm(ks[15], (L, MOE_EXPERTS, F, D), F ** -0.5 * beta),
        "ws_gate": nrm(ks[16], (L, D, MOE_SHARED_FFN), D ** -0.5),
        "ws_up": nrm(ks[17], (L, D, MOE_SHARED_FFN), D ** -0.5),
        "ws_down": nrm(ks[18], (L, MOE_SHARED_FFN, D), MOE_SHARED_FFN ** -0.5 * beta),
        "ln2_g": 1.0 + nrm(ks[19], (L, D), 0.01),
        "ln2_b": nrm(ks[20], (L, D), 0.01),
    }


def reference(x, w_in, conv_a_w, conv_a_b, ln_a_g, ln_a_b, conv_c_w, w_branch, w_out,
              ln1_g, ln1_b, router_w, router_bias, w_gate, w_up, w_down,
              ws_gate, ws_up, ws_down, ln2_g, ln2_b):
    alpha = (2.0 * DEPTH) ** 0.25
    B, S, D = x.shape
    for l in range(DEPTH):
        mix = token_mixer(x, w_in[l], conv_a_w[l], conv_a_b[l], ln_a_g[l], ln_a_b[l],
                          conv_c_w[l], w_branch[l], w_out[l])
        x = layer_norm(alpha * x + mix, ln1_g[l], ln1_b[l])
        ffn = moe(x.reshape(B * S, D), router_w[l], router_bias[l], w_gate[l], w_up[l], w_down[l],
                  ws_gate[l], ws_up[l], ws_down[l]).reshape(B, S, D)
        x = layer_norm(alpha * x + ffn, ln2_g[l], ln2_b[l])
    return x
```

```python
import functools

import jax
import jax.numpy as jnp
from jax import lax
from jax.experimental import pallas as pl
from jax.experimental.pallas import tpu as pltpu

D_MODEL = 2048
BRANCH_W = D_MODEL // 2
N_BRANCHES = 3
CONV_A_WIDTH = 31
CONV_C_WIDTH = 3
ATT_DILATIONS = (1, 4, 16)
ATT_BLOCK = 128
ATT_HEADS = 8
ATT_HEAD_DIM = BRANCH_W // ATT_HEADS
ATT_W = len(ATT_DILATIONS) * BRANCH_W
OFF_Q = 2 * BRANCH_W
OFF_K = OFF_Q + ATT_W
OFF_V = OFF_K + ATT_W
OFF_C = OFF_V + ATT_W
OFF_G = OFF_C + 3 * BRANCH_W
N_IN = OFF_G + N_BRANCHES * D_MODEL
MOE_EXPERTS = 64
MOE_TOPK = 8
MOE_GROUPS = 8
MOE_GROUP_SIZE = MOE_EXPERTS // MOE_GROUPS
MOE_TOPK_GROUPS = 4
MOE_FFN = D_MODEL // 8
MOE_ROUTED_SCALE = 2.5
LN_EPS = 1e-5
NEG_INF = -1e30
DEPTH_ALPHA_BASE = 2.0

LANES = 128
VMEM_LIMIT_BYTES = 52 * 1024 * 1024
MOE_ROWS = 256
CONV_TS = 256
CONV_HALO = 32
CONV_CHUNK = 32
COMBINE_TOK = 64


def _cparams(*sem):
    return pltpu.CompilerParams(dimension_semantics=sem, vmem_limit_bytes=VMEM_LIMIT_BYTES)


def _sigmoid(v):
    return 1.0 / (1.0 + jnp.exp(-v))


def _silu(v):
    return v * _sigmoid(v)


def _layer_norm_rows(v, g, b):
    mu = jnp.mean(v, axis=-1, keepdims=True)
    vc = v - mu
    var = jnp.mean(vc * vc, axis=-1, keepdims=True)
    return vc * lax.rsqrt(var + LN_EPS) * g + b


def _matmul_kernel(x_ref, w_ref, o_ref):
    o_ref[...] = jnp.dot(x_ref[...], w_ref[...], preferred_element_type=jnp.float32).astype(o_ref.dtype)


def _matmul(x, w, *, tm, tn, out_dtype):
    m, k = x.shape
    _, n = w.shape
    return pl.pallas_call(
        _matmul_kernel,
        grid=(m // tm, n // tn),
        in_specs=[pl.BlockSpec((tm, k), lambda i, j: (i, 0)),
                  pl.BlockSpec((k, tn), lambda i, j: (0, j))],
        out_specs=pl.BlockSpec((tm, tn), lambda i, j: (i, j)),
        out_shape=jax.ShapeDtypeStruct((m, n), out_dtype),
        compiler_params=_cparams("parallel", "parallel"),
        name="in_proj",
    )(x, w)


def _conv_kernel(val_ref, gate_ref, hval_ref, hgate_ref, cb_ref, cc_ref, ch_ref, hcc_ref, hch_ref,
                 wa_ref, ba_ref, ga_ref, bta_ref, wc_ref, a_ref, c_ref, abuf, cbuf):
    first = pl.program_id(1) == 0
    f32 = jnp.float32
    glu_h = hval_ref[0].astype(f32) * _sigmoid(hgate_ref[0].astype(f32))
    abuf[0:CONV_HALO, :] = jnp.where(first, 0.0, glu_h)
    abuf[CONV_HALO:, :] = val_ref[0].astype(f32) * _sigmoid(gate_ref[0].astype(f32))
    prod_h = hcc_ref[0].astype(f32) * hch_ref[0].astype(f32)
    cbuf[0:CONV_HALO, :] = jnp.where(first, 0.0, prod_h)
    cbuf[CONV_HALO:, :] = cc_ref[0].astype(f32) * ch_ref[0].astype(f32)

    a_off = CONV_HALO - (CONV_A_WIDTH - 1)
    c_off = CONV_HALO - (CONV_C_WIDTH - 1)
    for r0 in range(0, CONV_TS, CONV_CHUNK):
        acc = jnp.zeros((CONV_CHUNK, BRANCH_W), f32)
        for k in range(CONV_A_WIDTH):
            acc = acc + wa_ref[k:k + 1, :] * abuf[r0 + a_off + k:r0 + a_off + k + CONV_CHUNK, :]
        y = _layer_norm_rows(acc + ba_ref[...], ga_ref[...], bta_ref[...])
        a_ref[0, r0:r0 + CONV_CHUNK, :] = _silu(y).astype(a_ref.dtype)
        cacc = jnp.zeros((CONV_CHUNK, BRANCH_W), f32)
        for k in range(CONV_C_WIDTH):
            cacc = cacc + wc_ref[k:k + 1, :] * cbuf[r0 + c_off + k:r0 + c_off + k + CONV_CHUNK, :]
        c_ref[0, r0:r0 + CONV_CHUNK, :] = (cb_ref[0, r0:r0 + CONV_CHUNK, :].astype(f32) * cacc).astype(c_ref.dtype)


def _conv_branches(u, conv_a_w, conv_a_b, ln_a_g, ln_a_b, conv_c_w):
    b, s, _ = u.shape
    ts, halo = CONV_TS, CONV_HALO
    w = BRANCH_W
    ratio = ts // halo

    def main(col):
        return pl.BlockSpec((1, ts, w), lambda bi, i: (bi, i, col))

    def prev(col):
        return pl.BlockSpec((1, halo, w), lambda bi, i: (bi, jnp.maximum(i * ratio - 1, 0), col))

    def full(rows):
        return pl.BlockSpec((rows, w), lambda bi, i: (0, 0))

    cblk = OFF_C // w
    out = pl.pallas_call(
        _conv_kernel,
        grid=(b, s // ts),
        in_specs=[main(0), main(1), prev(0), prev(1),
                  main(cblk), main(cblk + 1), main(cblk + 2), prev(cblk + 1), prev(cblk + 2),
                  full(CONV_A_WIDTH), full(1), full(1), full(1), full(CONV_C_WIDTH)],
        out_specs=[pl.BlockSpec((1, ts, w), lambda bi, i: (bi, i, 0)),
                   pl.BlockSpec((1, ts, w), lambda bi, i: (bi, i, 0))],
        out_shape=[jax.ShapeDtypeStruct((b, s, w), jnp.bfloat16),
                   jax.ShapeDtypeStruct((b, s, w), jnp.bfloat16)],
        scratch_shapes=[pltpu.VMEM((halo + ts, w), jnp.float32),
                        pltpu.VMEM((halo + ts, w), jnp.float32)],
        compiler_params=_cparams("parallel", "parallel"),
        name="conv_branches",
    )(u, u, u, u, u, u, u, u, u,
      conv_a_w, conv_a_b.reshape(1, w), ln_a_g.reshape(1, w), ln_a_b.reshape(1, w), conv_c_w)
    return out


def _attn_kernel(q0, q1, q2, k0, k1, k2, v0, v1, v2, o_ref, q32, k32, v32, nbuf, mbuf, zbuf, *, seq):
    f32, bf16 = jnp.float32, jnp.bfloat16
    blk = ATT_BLOCK
    scale = ATT_HEAD_DIM ** -0.5
    qi = lax.broadcasted_iota(jnp.int32, (blk, 2 * blk), 0)
    ki = lax.broadcasted_iota(jnp.int32, (blk, 2 * blk), 1)
    dist = blk + qi - ki
    valid2 = (dist >= 0) & (dist <= blk)
    valid1 = (lax.broadcasted_iota(jnp.int32, (blk, blk), 0)
              >= lax.broadcasted_iota(jnp.int32, (blk, blk), 1))

    for g, (r, qr, kr, vr) in enumerate(zip(ATT_DILATIONS, (q0, q1, q2), (k0, k1, k2), (v0, v1, v2))):
        length = seq // r
        nblk = length // blk
        if r > 1:
            q32[...] = qr[0].astype(f32)
            k32[...] = kr[0].astype(f32)
            v32[...] = vr[0].astype(f32)
        for c in range(r):
            for n in range(nblk):
                q_start = c + n * blk * r
                k_start = c + (n - 1) * blk * r if n > 0 else q_start
                k_len = 2 * blk if n > 0 else blk
                if r == 1:
                    q = qr[0, q_start:q_start + blk, :]
                    k = kr[0, k_start:k_start + k_len, :]
                    v = vr[0, k_start:k_start + k_len, :]
                else:
                    q = q32[pl.ds(q_start, blk, stride=r), :].astype(bf16)
                    k = k32[pl.ds(k_start, k_len, stride=r), :].astype(bf16)
                    v = v32[pl.ds(k_start, k_len, stride=r), :].astype(bf16)
                s = lax.dot_general(q, k, (((1,), (1,)), ((), ())), preferred_element_type=f32) * scale
                s = jnp.where(valid2 if n > 0 else valid1, s, NEG_INF)
                m = jnp.max(s, axis=-1, keepdims=True)
                p = jnp.exp(s - m)
                z = jnp.sum(p, axis=-1, keepdims=True)
                num = jnp.dot(p.astype(bf16), v, preferred_element_type=f32)
                if r == 1:
                    rows = pl.ds(q_start, blk)
                else:
                    rows = pl.ds(q_start, blk, stride=r)
                nbuf[g, rows, :] = num
                mbuf[g, rows, :] = jnp.broadcast_to(m, (blk, ATT_HEAD_DIM))
                zbuf[g, rows, :] = jnp.broadcast_to(z, (blk, ATT_HEAD_DIM))

    rows_per = 256
    for r0 in range(0, seq, rows_per):
        sl = slice(r0, r0 + rows_per)
        m_all = jnp.maximum(jnp.maximum(mbuf[0, sl, :], mbuf[1, sl, :]), mbuf[2, sl, :])
        num = jnp.zeros((rows_per, ATT_HEAD_DIM), f32)
        den = jnp.zeros((rows_per, ATT_HEAD_DIM), f32)
        for g in range(len(ATT_DILATIONS)):
            e = jnp.exp(mbuf[g, sl, :] - m_all)
            num = num + e * nbuf[g, sl, :]
            den = den + e * zbuf[g, sl, :]
        o_ref[0, sl, :] = (num / den).astype(o_ref.dtype)


def _attention(u):
    b, s, _ = u.shape
    dh = ATT_HEAD_DIM
    n_groups = len(ATT_DILATIONS)

    def spec(off, g):
        base = (off + g * BRANCH_W) // dh
        return pl.BlockSpec((1, s, dh), lambda bi, h: (bi, 0, base + h))

    in_specs = ([spec(OFF_Q, g) for g in range(n_groups)] + [spec(OFF_K, g) for g in range(n_groups)]
                + [spec(OFF_V, g) for g in range(n_groups)])
    return pl.pallas_call(
        functools.partial(_attn_kernel, seq=s),
        grid=(b, ATT_HEADS),
        in_specs=in_specs,
        out_specs=pl.BlockSpec((1, s, dh), lambda bi, h: (bi, 0, h)),
        out_shape=jax.ShapeDtypeStruct((b, s, BRANCH_W), jnp.bfloat16),
        scratch_shapes=[pltpu.VMEM((s, dh), jnp.float32)] * 3
        + [pltpu.VMEM((n_groups, s, dh), jnp.float32)] * 3,
        compiler_params=_cparams("parallel", "parallel"),
        name="dilated_attention",
    )(*([u] * 9))


def _merge_kernel(a_ref, att_ref, c_ref, g0_ref, g1_ref, g2_ref, wb_ref, o_ref):
    f32 = jnp.float32
    acc = None
    for n, (br, gr) in enumerate(((a_ref, g0_ref), (att_ref, g1_ref), (c_ref, g2_ref))):
        proj = jnp.dot(br[...], wb_ref[n], preferred_element_type=f32)
        term = _sigmoid(gr[...].astype(f32)) * proj
        acc = term if acc is None else acc + term
    o_ref[...] = acc.astype(o_ref.dtype)


def _branch_merge(a, att, c, u2d, w_branch, *, tm=512, tn=1024):
    t, w = a.shape
    d = D_MODEL
    gblk = OFF_G // tn

    def gate(n):
        return pl.BlockSpec((tm, tn), lambda j, i: (i, gblk + n * (d // tn) + j))

    row = pl.BlockSpec((tm, w), lambda j, i: (i, 0))
    return pl.pallas_call(
        _merge_kernel,
        grid=(d // tn, t // tm),
        in_specs=[row, row, row, gate(0), gate(1), gate(2),
                  pl.BlockSpec((N_BRANCHES, w, tn), lambda j, i: (0, 0, j))],
        out_specs=pl.BlockSpec((tm, tn), lambda j, i: (i, j)),
        out_shape=jax.ShapeDtypeStruct((t, d), jnp.bfloat16),
        compiler_params=_cparams("parallel", "parallel"),
        name="branch_merge",
    )(a, att, c, u2d, u2d, u2d, w_branch)


def _outproj_ln_kernel(m_ref, w_ref, x_ref, g_ref, b_ref, o_ref, *, alpha):
    y = jnp.dot(m_ref[...], w_ref[...], preferred_element_type=jnp.float32)
    o_ref[...] = _layer_norm_rows(alpha * x_ref[...] + y, g_ref[...], b_ref[...])


def _outproj_ln(merged, w_out, x, g, b, *, alpha, tm=512):
    t, d = x.shape
    row = lambda i: (i, 0)
    const = lambda i: (0, 0)
    return pl.pallas_call(
        functools.partial(_outproj_ln_kernel, alpha=alpha),
        grid=(t // tm,),
        in_specs=[pl.BlockSpec((tm, d), row), pl.BlockSpec((d, d), const), pl.BlockSpec((tm, d), row),
                  pl.BlockSpec((1, d), const), pl.BlockSpec((1, d), const)],
        out_specs=pl.BlockSpec((tm, d), row),
        out_shape=jax.ShapeDtypeStruct((t, d), jnp.float32),
        compiler_params=_cparams("parallel"),
        name="out_proj_ln",
    )(merged, w_out, x, g.reshape(1, d), b.reshape(1, d))


def _first_argmax(vals, iota, axis, size):
    mx = jnp.max(vals, axis=axis, keepdims=True)
    idx = jnp.min(jnp.where(vals == mx, iota, size), axis=axis, keepdims=True)
    return mx, idx


def _router_kernel(x_ref, wt_ref, bias_ref, idx_ref, wts_ref):
    f32 = jnp.float32
    e, gsz, ng = MOE_EXPERTS, MOE_GROUP_SIZE, MOE_GROUPS
    tm = x_ref.shape[0]
    logits = lax.dot_general(wt_ref[...], x_ref[...], (((1,), (1,)), ((), ())),
                             precision=lax.Precision.HIGHEST, preferred_element_type=f32)
    scores = _sigmoid(logits)
    choice = scores + bias_ref[...]
    grp = choice.reshape(ng, gsz, tm)
    iota_in = lax.broadcasted_iota(jnp.int32, (ng, gsz, tm), 1)
    m1, i1 = _first_argmax(grp, iota_in, 1, gsz)
    m2 = jnp.max(jnp.where(iota_in == i1, -jnp.inf, grp), axis=1, keepdims=True)
    grp_score = (m1 + m2).reshape(ng, tm)
    iota_g = lax.broadcasted_iota(jnp.int32, (ng, tm), 0)
    grp_mask = jnp.zeros((ng, tm), jnp.bool_)
    for _ in range(MOE_TOPK_GROUPS):
        _, gi = _first_argmax(grp_score, iota_g, 0, ng)
        hit = iota_g == gi
        grp_mask = grp_mask | hit
        grp_score = jnp.where(hit, -jnp.inf, grp_score)
    expert_mask = jnp.broadcast_to(grp_mask.reshape(ng, 1, tm), (ng, gsz, tm)).reshape(e, tm)
    masked = jnp.where(expert_mask, choice, NEG_INF)
    iota_e = lax.broadcasted_iota(jnp.int32, (e, tm), 0)
    idxs, ws = [], []
    for _ in range(MOE_TOPK):
        _, ei = _first_argmax(masked, iota_e, 0, e)
        hit = iota_e == ei
        idxs.append(ei)
        ws.append(jnp.sum(jnp.where(hit, scores, 0.0), axis=0, keepdims=True))
        masked = jnp.where(hit, -jnp.inf, masked)
    w_all = jnp.concatenate(ws, axis=0)
    idx_ref[...] = jnp.concatenate(idxs, axis=0)
    wts_ref[...] = w_all / jnp.sum(w_all, axis=0, keepdims=True) * MOE_ROUTED_SCALE


def _router(x1, router_w, router_bias, *, tm=512):
    t, d = x1.shape
    e = MOE_EXPERTS
    return pl.pallas_call(
        _router_kernel,
        grid=(t // tm,),
        in_specs=[pl.BlockSpec((tm, d), lambda i: (i, 0)),
                  pl.BlockSpec((e, d), lambda i: (0, 0)),
                  pl.BlockSpec((e, 1), lambda i: (0, 0))],
        out_specs=[pl.BlockSpec((MOE_TOPK, tm), lambda i: (0, i)),
                   pl.BlockSpec((MOE_TOPK, tm), lambda i: (0, i))],
        out_shape=[jax.ShapeDtypeStruct((MOE_TOPK, t), jnp.int32),
                   jax.ShapeDtypeStruct((MOE_TOPK, t), jnp.float32)],
        compiler_params=_cparams("parallel"),
        name="router_topk",
    )(x1, router_w.T, router_bias.reshape(e, 1).astype(jnp.float32))


def _expert_kernel(blk_e_ref, nvalid_ref, tok_ref, tok_next_ref, x_hbm, roww_ref, wgu_ref, wd_ref, y_ref,
                   xbuf, sem):
    i = pl.program_id(0)
    nvalid = nvalid_ref[0]
    slot = lax.rem(i, 2)

    def row_copy(tokens_ref, r, dst_slot):
        tok = tokens_ref[0, 0, r]
        return pltpu.make_async_copy(x_hbm.at[pl.ds(tok, 1), :], xbuf.at[dst_slot, pl.ds(r, 1), :],
                                     sem.at[dst_slot])

    def start_block(tokens_ref, dst_slot):
        def body(r, carry):
            row_copy(tokens_ref, r, dst_slot).start()
            return carry
        lax.fori_loop(0, MOE_ROWS, body, 0)

    @pl.when((i == 0) & (nvalid > 0))
    def _():
        start_block(tok_ref, 0)

    @pl.when(i + 1 < nvalid)
    def _():
        start_block(tok_next_ref, 1 - slot)

    @pl.when(i < nvalid)
    def _():
        pltpu.make_async_copy(x_hbm.at[pl.ds(0, MOE_ROWS), :], xbuf.at[slot], sem.at[slot]).wait()
        xb = xbuf[slot].astype(jnp.bfloat16)
        h = jnp.dot(xb, wgu_ref[...], preferred_element_type=jnp.float32)
        act = _silu(h[:, :MOE_FFN]) * h[:, MOE_FFN:]
        y = jnp.dot(act.astype(jnp.bfloat16), wd_ref[...], preferred_element_type=jnp.float32)
        y_ref[...] = y * roww_ref[...]

    @pl.when(i >= nvalid)
    def _():
        y_ref[...] = jnp.zeros_like(y_ref)


def _routed_experts(x1, blk_expert, nvalid, row_tok, row_w, wgu, wd):
    t, d = x1.shape
    nb = blk_expert.shape[0]
    rows = MOE_ROWS
    tok3 = row_tok.reshape(nb, 1, rows)
    smem_blk = lambda f: pl.BlockSpec((1, 1, rows), f, memory_space=pltpu.SMEM)
    grid_spec = pltpu.PrefetchScalarGridSpec(
        num_scalar_prefetch=2,
        grid=(nb,),
        in_specs=[smem_blk(lambda i, be, nv: (i, 0, 0)),
                  smem_blk(lambda i, be, nv: (jnp.minimum(i + 1, nb - 1), 0, 0)),
                  pl.BlockSpec(memory_space=pl.ANY),
                  pl.BlockSpec((rows, 1), lambda i, be, nv: (i, 0)),
                  pl.BlockSpec((None, d, 2 * MOE_FFN), lambda i, be, nv: (be[i], 0, 0)),
                  pl.BlockSpec((None, MOE_FFN, d), lambda i, be, nv: (be[i], 0, 0))],
        out_specs=pl.BlockSpec((rows, d), lambda i, be, nv: (i, 0)),
        scratch_shapes=[pltpu.VMEM((2, rows, d), jnp.float32), pltpu.SemaphoreType.DMA((2,))],
    )
    return pl.pallas_call(
        _expert_kernel,
        grid_spec=grid_spec,
        out_shape=jax.ShapeDtypeStruct((nb * rows, d), jnp.float32),
        compiler_params=_cparams("arbitrary"),
        name="routed_experts",
    )(blk_expert, nvalid, tok3, tok3, x1, row_w.reshape(nb * rows, 1), wgu, wd)


def _shared_kernel(x_ref, wgu_ref, wd_ref, o_ref):
    xb = x_ref[...].astype(jnp.bfloat16)
    h = jnp.dot(xb, wgu_ref[...], preferred_element_type=jnp.float32)
    act = _silu(h[:, :MOE_FFN]) * h[:, MOE_FFN:]
    o_ref[...] = jnp.dot(act.astype(jnp.bfloat16), wd_ref[...], preferred_element_type=jnp.float32)


def _shared_expert(x1, wgu, wd, *, tm=512):
    t, d = x1.shape
    return pl.pallas_call(
        _shared_kernel,
        grid=(t // tm,),
        in_specs=[pl.BlockSpec((tm, d), lambda i: (i, 0)),
                  pl.BlockSpec((d, 2 * MOE_FFN), lambda i: (0, 0)),
                  pl.BlockSpec((MOE_FFN, d), lambda i: (0, 0))],
        out_specs=pl.BlockSpec((tm, d), lambda i: (i, 0)),
        out_shape=jax.ShapeDtypeStruct((t, d), jnp.float32),
        compiler_params=_cparams("parallel"),
        name="shared_expert",
    )(x1, wgu, wd)


def _combine_kernel(dest_ref, dest_next_ref, ys_hbm, x_ref, sh_ref, g_ref, b_ref, o_ref, ybuf, sem, *, alpha, nsteps):
    i = pl.program_id(0)
    slot = lax.rem(i, 2)
    n_rows = COMBINE_TOK * MOE_TOPK

    def start_block(d_ref, dst_slot):
        def body(r, carry):
            row = d_ref[0, 0, r]
            pltpu.make_async_copy(ys_hbm.at[pl.ds(row, 1), :], ybuf.at[dst_slot, pl.ds(r, 1), :],
                                  sem.at[dst_slot]).start()
            return carry
        lax.fori_loop(0, n_rows, body, 0)

    @pl.when(i == 0)
    def _():
        start_block(dest_ref, 0)

    @pl.when(i + 1 < nsteps)
    def _():
        start_block(dest_next_ref, 1 - slot)

    pltpu.make_async_copy(ys_hbm.at[pl.ds(0, n_rows), :], ybuf.at[slot], sem.at[slot]).wait()
    acc = alpha * x_ref[...] + sh_ref[...]
    for k in range(MOE_TOPK):
        acc = acc + ybuf[slot, k * COMBINE_TOK:(k + 1) * COMBINE_TOK, :]
    o_ref[...] = _layer_norm_rows(acc, g_ref[...], b_ref[...])


def _combine_ln(ys, dest_kt, x1, shared, g, b, *, alpha):
    t, d = x1.shape
    tc = COMBINE_TOK
    nsteps = t // tc
    dest3 = dest_kt.reshape(MOE_TOPK, nsteps, tc).transpose(1, 0, 2).reshape(nsteps, 1, MOE_TOPK * tc)
    smem_blk = lambda f: pl.BlockSpec((1, 1, MOE_TOPK * tc), f, memory_space=pltpu.SMEM)
    row = lambda i: (i, 0)
    const = lambda i: (0, 0)
    return pl.pallas_call(
        functools.partial(_combine_kernel, alpha=alpha, nsteps=nsteps),
        grid=(nsteps,),
        in_specs=[smem_blk(lambda i: (i, 0, 0)),
                  smem_blk(lambda i: (jnp.minimum(i + 1, nsteps - 1), 0, 0)),
                  pl.BlockSpec(memory_space=pl.ANY),
                  pl.BlockSpec((tc, d), row), pl.BlockSpec((tc, d), row),
                  pl.BlockSpec((1, d), const), pl.BlockSpec((1, d), const)],
        out_specs=pl.BlockSpec((tc, d), row),
        out_shape=jax.ShapeDtypeStruct((t, d), jnp.float32),
        scratch_shapes=[pltpu.VMEM((2, MOE_TOPK * tc, d), jnp.float32), pltpu.SemaphoreType.DMA((2,))],
        compiler_params=_cparams("arbitrary"),
        name="combine_ln",
    )(dest3, dest3, ys, x1, shared, g.reshape(1, d), b.reshape(1, d))


def _dispatch_plan(idx_kt, wts_kt):
    k, t = idx_kt.shape
    e, rows = MOE_EXPERTS, MOE_ROWS
    tk = k * t
    nb = tk // rows + e
    flat_e = idx_kt.reshape(-1)
    order = jnp.argsort(flat_e, stable=True).astype(jnp.int32)
    counts = jnp.bincount(flat_e, length=e).astype(jnp.int32)
    padded = (counts + rows - 1) // rows * rows
    pend = jnp.cumsum(padded)
    pstart = pend - padded
    start = jnp.cumsum(counts) - counts
    blk_expert = jnp.clip(jnp.searchsorted(pend, jnp.arange(nb, dtype=jnp.int32) * rows, side="right"),
                          0, e - 1).astype(jnp.int32)
    nvalid = (pend[-1] // rows).astype(jnp.int32).reshape(1)
    j = jnp.arange(nb * rows, dtype=jnp.int32)
    e_j = jnp.repeat(blk_expert, rows)
    off = j - pstart[e_j]
    valid = (off < counts[e_j]) & (j < pend[-1])
    pair = order[jnp.clip(start[e_j] + off, 0, tk - 1)]
    row_tok = jnp.where(valid, pair % t, 0).astype(jnp.int32)
    row_w = jnp.where(valid, wts_kt.reshape(-1)[pair], 0.0)
    rank = jnp.zeros((tk,), jnp.int32).at[order].set(jnp.arange(tk, dtype=jnp.int32))
    dest = (pstart[flat_e] + rank - start[flat_e]).astype(jnp.int32).reshape(k, t)
    return blk_expert, nvalid, row_tok, row_w, dest


def kernel(x, w_in, conv_a_w, conv_a_b, ln_a_g, ln_a_b, conv_c_w, w_branch, w_out, ln1_g, ln1_b, router_w,
           router_bias, w_gate, w_up, w_down, ws_gate, ws_up, ws_down, ln2_g, ln2_b):
    b, s, d = x.shape
    depth = w_in.shape[0]
    t = b * s
    alpha = (DEPTH_ALPHA_BASE * depth) ** 0.25
    bf16 = jnp.bfloat16
    xf = x.reshape(t, d)
    for l in range(depth):
        u = _matmul(xf.astype(bf16), w_in[l].astype(bf16), tm=min(t, 1024), tn=1024, out_dtype=bf16)
        u3 = u.reshape(b, s, N_IN)
        a, c = _conv_branches(u3, conv_a_w[l], conv_a_b[l], ln_a_g[l], ln_a_b[l], conv_c_w[l])
        att = _attention(u3)
        merged = _branch_merge(a.reshape(t, BRANCH_W), att.reshape(t, BRANCH_W), c.reshape(t, BRANCH_W), u,
                               w_branch[l].astype(bf16))
        x1 = _outproj_ln(merged, w_out[l].astype(bf16), xf, ln1_g[l], ln1_b[l], alpha=alpha)
        idx_kt, wts_kt = _router(x1, router_w[l], router_bias[l])
        blk_expert, nvalid, row_tok, row_w, dest = _dispatch_plan(idx_kt, wts_kt)
        wgu = jnp.concatenate([w_gate[l], w_up[l]], axis=-1).astype(bf16)
        ys = _routed_experts(x1, blk_expert, nvalid, row_tok, row_w, wgu, w_down[l].astype(bf16))
        wsgu = jnp.concatenate([ws_gate[l], ws_up[l]], axis=-1).astype(bf16)
        shared = _shared_expert(x1, wsgu, ws_down[l].astype(bf16))
        xf = _combine_ln(ys, dest, x1, shared, ln2_g[l], ln2_b[l], alpha=alpha)
    return xf.reshape(b, s, d)
```

```python
import functools

import jax
import jax.numpy as jnp
from jax import lax
from jax.experimental import pallas as pl
from jax.experimental.pallas import tpu as pltpu

D_MODEL = 2048
BRANCH_W = D_MODEL // 2
N_BRANCHES = 3
CONV_A_WIDTH = 31
CONV_C_WIDTH = 3
ATT_DILATIONS = (1, 4, 16)
ATT_BLOCK = 128
ATT_HEADS = 8
ATT_HEAD_DIM = BRANCH_W // ATT_HEADS
ATT_W = len(ATT_DILATIONS) * BRANCH_W
OFF_Q = 2 * BRANCH_W
OFF_K = OFF_Q + ATT_W
OFF_V = OFF_K + ATT_W
OFF_C = OFF_V + ATT_W
OFF_G = OFF_C + 3 * BRANCH_W
N_IN = OFF_G + N_BRANCHES * D_MODEL
MOE_EXPERTS = 64
MOE_TOPK = 8
MOE_GROUPS = 8
MOE_GROUP_SIZE = MOE_EXPERTS // MOE_GROUPS
MOE_TOPK_GROUPS = 4
MOE_FFN = D_MODEL // 8
MOE_ROUTED_SCALE = 2.5
LN_EPS = 1e-5
NEG_INF = -1e30
DEPTH_ALPHA_BASE = 2.0

LANES = 128
VMEM_LIMIT_BYTES = 52 * 1024 * 1024
MOE_ROWS = 256
PACK_SUBLANES = D_MODEL // 2 // LANES
ROW_SUBLANES = D_MODEL // LANES
HIGH_HALF_MASK = -65536
CONV_TS = 256
CONV_HALO = 32
CONV_CHUNK = 32
COMBINE_TOK = 64


def _cparams(*sem):
    return pltpu.CompilerParams(dimension_semantics=sem, vmem_limit_bytes=VMEM_LIMIT_BYTES)


def _sigmoid(v):
    return 1.0 / (1.0 + jnp.exp(-v))


def _silu(v):
    return v * _sigmoid(v)


def _layer_norm_rows(v, g, b):
    mu = jnp.mean(v, axis=-1, keepdims=True)
    vc = v - mu
    var = jnp.mean(vc * vc, axis=-1, keepdims=True)
    return vc * lax.rsqrt(var + LN_EPS) * g + b


def _matmul_kernel(x_ref, w_ref, o_ref):
    o_ref[...] = jnp.dot(x_ref[...], w_ref[...], preferred_element_type=jnp.float32).astype(o_ref.dtype)


def _matmul(x, w, *, tm, tn, out_dtype):
    m, k = x.shape
    _, n = w.shape
    return pl.pallas_call(
        _matmul_kernel,
        grid=(m // tm, n // tn),
        in_specs=[pl.BlockSpec((tm, k), lambda i, j: (i, 0)),
                  pl.BlockSpec((k, tn), lambda i, j: (0, j))],
        out_specs=pl.BlockSpec((tm, tn), lambda i, j: (i, j)),
        out_shape=jax.ShapeDtypeStruct((m, n), out_dtype),
        compiler_params=_cparams("parallel", "parallel"),
        name="in_proj",
    )(x, w)


def _conv_kernel(val_ref, gate_ref, hval_ref, hgate_ref, cb_ref, cc_ref, ch_ref, hcc_ref, hch_ref,
                 wa_ref, ba_ref, ga_ref, bta_ref, wc_ref, a_ref, c_ref, abuf, cbuf):
    first = pl.program_id(1) == 0
    f32 = jnp.float32
    glu_h = hval_ref[0].astype(f32) * _sigmoid(hgate_ref[0].astype(f32))
    abuf[0:CONV_HALO, :] = jnp.where(first, 0.0, glu_h)
    abuf[CONV_HALO:, :] = val_ref[0].astype(f32) * _sigmoid(gate_ref[0].astype(f32))
    prod_h = hcc_ref[0].astype(f32) * hch_ref[0].astype(f32)
    cbuf[0:CONV_HALO, :] = jnp.where(first, 0.0, prod_h)
    cbuf[CONV_HALO:, :] = cc_ref[0].astype(f32) * ch_ref[0].astype(f32)

    a_off = CONV_HALO - (CONV_A_WIDTH - 1)
    c_off = CONV_HALO - (CONV_C_WIDTH - 1)
    for r0 in range(0, CONV_TS, CONV_CHUNK):
        acc = jnp.zeros((CONV_CHUNK, BRANCH_W), f32)
        for k in range(CONV_A_WIDTH):
            acc = acc + wa_ref[k:k + 1, :] * abuf[r0 + a_off + k:r0 + a_off + k + CONV_CHUNK, :]
        y = _layer_norm_rows(acc + ba_ref[...], ga_ref[...], bta_ref[...])
        a_ref[0, r0:r0 + CONV_CHUNK, :] = _silu(y).astype(a_ref.dtype)
        cacc = jnp.zeros((CONV_CHUNK, BRANCH_W), f32)
        for k in range(CONV_C_WIDTH):
            cacc = cacc + wc_ref[k:k + 1, :] * cbuf[r0 + c_off + k:r0 + c_off + k + CONV_CHUNK, :]
        c_ref[0, r0:r0 + CONV_CHUNK, :] = (cb_ref[0, r0:r0 + CONV_CHUNK, :].astype(f32) * cacc).astype(c_ref.dtype)


def _conv_branches(u, conv_a_w, conv_a_b, ln_a_g, ln_a_b, conv_c_w):
    b, s, _ = u.shape
    ts, halo = CONV_TS, CONV_HALO
    w = BRANCH_W
    ratio = ts // halo

    def main(col):
        return pl.BlockSpec((1, ts, w), lambda bi, i: (bi, i, col))

    def prev(col):
        return pl.BlockSpec((1, halo, w), lambda bi, i: (bi, jnp.maximum(i * ratio - 1, 0), col))

    def full(rows):
        return pl.BlockSpec((rows, w), lambda bi, i: (0, 0))

    cblk = OFF_C // w
    out = pl.pallas_call(
        _conv_kernel,
        grid=(b, s // ts),
        in_specs=[main(0), main(1), prev(0), prev(1),
                  main(cblk), main(cblk + 1), main(cblk + 2), prev(cblk + 1), prev(cblk + 2),
                  full(CONV_A_WIDTH), full(1), full(1), full(1), full(CONV_C_WIDTH)],
        out_specs=[pl.BlockSpec((1, ts, w), lambda bi, i: (bi, i, 0)),
                   pl.BlockSpec((1, ts, w), lambda bi, i: (bi, i, 0))],
        out_shape=[jax.ShapeDtypeStruct((b, s, w), jnp.bfloat16),
                   jax.ShapeDtypeStruct((b, s, w), jnp.bfloat16)],
        scratch_shapes=[pltpu.VMEM((halo + ts, w), jnp.float32),
                        pltpu.VMEM((halo + ts, w), jnp.float32)],
        compiler_params=_cparams("parallel", "parallel"),
        name="conv_branches",
    )(u, u, u, u, u, u, u, u, u,
      conv_a_w, conv_a_b.reshape(1, w), ln_a_g.reshape(1, w), ln_a_b.reshape(1, w), conv_c_w)
    return out


def _attn_kernel(q0, q1, q2, k0, k1, k2, v0, v1, v2, o_ref, q32, k32, v32, nbuf, mbuf, zbuf, *, seq):
    f32, bf16 = jnp.float32, jnp.bfloat16
    blk = ATT_BLOCK
    scale = ATT_HEAD_DIM ** -0.5
    qi = lax.broadcasted_iota(jnp.int32, (blk, 2 * blk), 0)
    ki = lax.broadcasted_iota(jnp.int32, (blk, 2 * blk), 1)
    dist = blk + qi - ki
    valid2 = (dist >= 0) & (dist <= blk)
    valid1 = (lax.broadcasted_iota(jnp.int32, (blk, blk), 0)
              >= lax.broadcasted_iota(jnp.int32, (blk, blk), 1))

    for g, (r, qr, kr, vr) in enumerate(zip(ATT_DILATIONS, (q0, q1, q2), (k0, k1, k2), (v0, v1, v2))):
        length = seq // r
        nblk = length // blk
        if r > 1:
            q32[...] = qr[0].astype(f32)
            k32[...] = kr[0].astype(f32)
            v32[...] = vr[0].astype(f32)
        for c in range(r):
            for n in range(nblk):
                q_start = c + n * blk * r
                k_start = c + (n - 1) * blk * r if n > 0 else q_start
                k_len = 2 * blk if n > 0 else blk
                if r == 1:
                    q = qr[0, q_start:q_start + blk, :]
                    k = kr[0, k_start:k_start + k_len, :]
                    v = vr[0, k_start:k_start + k_len, :]
                else:
                    q = q32[pl.ds(q_start, blk, stride=r), :].astype(bf16)
                    k = k32[pl.ds(k_start, k_len, stride=r), :].astype(bf16)
                    v = v32[pl.ds(k_start, k_len, stride=r), :].astype(bf16)
                s = lax.dot_general(q, k, (((1,), (1,)), ((), ())), preferred_element_type=f32) * scale
                s = jnp.where(valid2 if n > 0 else valid1, s, NEG_INF)
                m = jnp.max(s, axis=-1, keepdims=True)
                p = jnp.exp(s - m)
                z = jnp.sum(p, axis=-1, keepdims=True)
                num = jnp.dot(p.astype(bf16), v, preferred_element_type=f32)
                if r == 1:
                    rows = pl.ds(q_start, blk)
                else:
                    rows = pl.ds(q_start, blk, stride=r)
                nbuf[g, rows, :] = num
                mbuf[g, rows, :] = jnp.broadcast_to(m, (blk, ATT_HEAD_DIM))
                zbuf[g, rows, :] = jnp.broadcast_to(z, (blk, ATT_HEAD_DIM))

    rows_per = 256
    for r0 in range(0, seq, rows_per):
        sl = slice(r0, r0 + rows_per)
        m_all = jnp.maximum(jnp.maximum(mbuf[0, sl, :], mbuf[1, sl, :]), mbuf[2, sl, :])
        num = jnp.zeros((rows_per, ATT_HEAD_DIM), f32)
        den = jnp.zeros((rows_per, ATT_HEAD_DIM), f32)
        for g in range(len(ATT_DILATIONS)):
            e = jnp.exp(mbuf[g, sl, :] - m_all)
            num = num + e * nbuf[g, sl, :]
            den = den + e * zbuf[g, sl, :]
        o_ref[0, sl, :] = (num / den).astype(o_ref.dtype)


def _attention(u):
    b, s, _ = u.shape
    dh = ATT_HEAD_DIM
    n_groups = len(ATT_DILATIONS)

    def spec(off, g):
        base = (off + g * BRANCH_W) // dh
        return pl.BlockSpec((1, s, dh), lambda bi, h: (bi, 0, base + h))

    in_specs = ([spec(OFF_Q, g) for g in range(n_groups)] + [spec(OFF_K, g) for g in range(n_groups)]
                + [spec(OFF_V, g) for g in range(n_groups)])
    return pl.pallas_call(
        functools.partial(_attn_kernel, seq=s),
        grid=(b, ATT_HEADS),
        in_specs=in_specs,
        out_specs=pl.BlockSpec((1, s, dh), lambda bi, h: (bi, 0, h)),
        out_shape=jax.ShapeDtypeStruct((b, s, BRANCH_W), jnp.bfloat16),
        scratch_shapes=[pltpu.VMEM((s, dh), jnp.float32)] * 3
        + [pltpu.VMEM((n_groups, s, dh), jnp.float32)] * 3,
        compiler_params=_cparams("parallel", "parallel"),
        name="dilated_attention",
    )(*([u] * 9))


def _merge_kernel(a_ref, att_ref, c_ref, g0_ref, g1_ref, g2_ref, wb_ref, o_ref):
    f32 = jnp.float32
    acc = None
    for n, (br, gr) in enumerate(((a_ref, g0_ref), (att_ref, g1_ref), (c_ref, g2_ref))):
        proj = jnp.dot(br[...], wb_ref[n], preferred_element_type=f32)
        term = _sigmoid(gr[...].astype(f32)) * proj
        acc = term if acc is None else acc + term
    o_ref[...] = acc.astype(o_ref.dtype)


def _branch_merge(a, att, c, u2d, w_branch, *, tm=512, tn=1024):
    t, w = a.shape
    d = D_MODEL
    gblk = OFF_G // tn

    def gate(n):
        return pl.BlockSpec((tm, tn), lambda j, i: (i, gblk + n * (d // tn) + j))

    row = pl.BlockSpec((tm, w), lambda j, i: (i, 0))
    return pl.pallas_call(
        _merge_kernel,
        grid=(d // tn, t // tm),
        in_specs=[row, row, row, gate(0), gate(1), gate(2),
                  pl.BlockSpec((N_BRANCHES, w, tn), lambda j, i: (0, 0, j))],
        out_specs=pl.BlockSpec((tm, tn), lambda j, i: (i, j)),
        out_shape=jax.ShapeDtypeStruct((t, d), jnp.bfloat16),
        compiler_params=_cparams("parallel", "parallel"),
        name="branch_merge",
    )(a, att, c, u2d, u2d, u2d, w_branch)


def _pack_rows(y, out_ref, rows):
    half = D_MODEL // 2
    lo = lax.bitcast_convert_type(y[:, :half].astype(jnp.bfloat16).astype(jnp.float32), jnp.int32)
    hi = lax.bitcast_convert_type(y[:, half:].astype(jnp.bfloat16).astype(jnp.float32), jnp.int32)
    word = (hi & HIGH_HALF_MASK) | lax.shift_right_logical(lo, 16)
    for c in range(PACK_SUBLANES):
        out_ref[pl.ds(c, rows, stride=PACK_SUBLANES), :] = word[:, c * LANES:(c + 1) * LANES]


def _unpack_rows(in_ref, rows):
    los, his = [], []
    for c in range(PACK_SUBLANES):
        word = in_ref[pl.ds(c, rows, stride=PACK_SUBLANES), :]
        los.append(lax.bitcast_convert_type(lax.shift_left(word, 16), jnp.float32).astype(jnp.bfloat16))
        his.append(lax.bitcast_convert_type(word & HIGH_HALF_MASK, jnp.float32).astype(jnp.bfloat16))
    return jnp.concatenate(los + his, axis=1)


def _outproj_ln_kernel(m_ref, w_ref, x_ref, g_ref, b_ref, o_ref, p_ref, *, alpha):
    y = jnp.dot(m_ref[...], w_ref[...], preferred_element_type=jnp.float32)
    x1 = _layer_norm_rows(alpha * x_ref[...] + y, g_ref[...], b_ref[...])
    o_ref[...] = x1
    _pack_rows(x1, p_ref, x1.shape[0])


def _outproj_ln(merged, w_out, x, g, b, *, alpha, tm=512):
    t, d = x.shape
    row = lambda i: (i, 0)
    const = lambda i: (0, 0)
    return pl.pallas_call(
        functools.partial(_outproj_ln_kernel, alpha=alpha),
        grid=(t // tm,),
        in_specs=[pl.BlockSpec((tm, d), row), pl.BlockSpec((d, d), const), pl.BlockSpec((tm, d), row),
                  pl.BlockSpec((1, d), const), pl.BlockSpec((1, d), const)],
        out_specs=[pl.BlockSpec((tm, d), row), pl.BlockSpec((tm * PACK_SUBLANES, LANES), row)],
        out_shape=[jax.ShapeDtypeStruct((t, d), jnp.float32),
                   jax.ShapeDtypeStruct((t * PACK_SUBLANES, LANES), jnp.int32)],
        compiler_params=_cparams("parallel"),
        name="out_proj_ln",
    )(merged, w_out, x, g.reshape(1, d), b.reshape(1, d))


def _first_argmax(vals, iota, axis, size):
    mx = jnp.max(vals, axis=axis, keepdims=True)
    idx = jnp.min(jnp.where(vals == mx, iota, size), axis=axis, keepdims=True)
    return mx, idx


def _router_kernel(x_ref, wt_ref, bias_ref, idx_ref, wts_ref, rank_ref, cnt_ref, carry):
    f32 = jnp.float32
    e, gsz, ng = MOE_EXPERTS, MOE_GROUP_SIZE, MOE_GROUPS
    tm = x_ref.shape[0]

    @pl.when(pl.program_id(0) == 0)
    def _():
        carry[...] = jnp.zeros_like(carry)

    logits = lax.dot_general(wt_ref[...], x_ref[...], (((1,), (1,)), ((), ())),
                             precision=lax.Precision.HIGHEST, preferred_element_type=f32)
    scores = _sigmoid(logits)
    choice = scores + bias_ref[...]
    grp = choice.reshape(ng, gsz, tm)
    iota_in = lax.broadcasted_iota(jnp.int32, (ng, gsz, tm), 1)
    m1, i1 = _first_argmax(grp, iota_in, 1, gsz)
    m2 = jnp.max(jnp.where(iota_in == i1, -jnp.inf, grp), axis=1, keepdims=True)
    grp_score = (m1 + m2).reshape(ng, tm)
    iota_g = lax.broadcasted_iota(jnp.int32, (ng, tm), 0)
    grp_mask = jnp.zeros((ng, tm), jnp.bool_)
    for _ in range(MOE_TOPK_GROUPS):
        _, gi = _first_argmax(grp_score, iota_g, 0, ng)
        hit = iota_g == gi
        grp_mask = grp_mask | hit
        grp_score = jnp.where(hit, -jnp.inf, grp_score)
    expert_mask = jnp.broadcast_to(grp_mask.reshape(ng, 1, tm), (ng, gsz, tm)).reshape(e, tm)
    masked = jnp.where(expert_mask, choice, NEG_INF)
    iota_e = lax.broadcasted_iota(jnp.int32, (e, tm), 0)
    idxs, ws, hits = [], [], []
    for _ in range(MOE_TOPK):
        _, ei = _first_argmax(masked, iota_e, 0, e)
        hit = iota_e == ei
        idxs.append(ei)
        hits.append(hit)
        ws.append(jnp.sum(jnp.where(hit, scores, 0.0), axis=0, keepdims=True))
        masked = jnp.where(hit, -jnp.inf, masked)
    w_all = jnp.concatenate(ws, axis=0)
    idx_ref[...] = jnp.concatenate(idxs, axis=0)
    wts_ref[...] = w_all / jnp.sum(w_all, axis=0, keepdims=True) * MOE_ROUTED_SCALE

    chosen = hits[0]
    for hit in hits[1:]:
        chosen = chosen | hit
    chosen_f = jnp.where(chosen, 1.0, 0.0)
    earlier = (lax.broadcasted_iota(jnp.int32, (tm, tm), 0) < lax.broadcasted_iota(jnp.int32, (tm, tm), 1))
    prefix = jnp.dot(chosen_f.astype(jnp.bfloat16), jnp.where(earlier, 1.0, 0.0).astype(jnp.bfloat16),
                     preferred_element_type=f32)
    base = carry[:, 0:1] + prefix
    ranks = [jnp.sum(jnp.where(hit, base, 0.0), axis=0, keepdims=True) for hit in hits]
    rank_ref[...] = jnp.concatenate(ranks, axis=0).astype(jnp.int32)
    carry[...] = carry[...] + jnp.sum(chosen_f, axis=1, keepdims=True)
    cnt_ref[...] = carry[...]


def _router(x1, router_w, router_bias, *, tm=512):
    t, d = x1.shape
    e = MOE_EXPERTS
    tok = pl.BlockSpec((MOE_TOPK, tm), lambda i: (0, i))
    return pl.pallas_call(
        _router_kernel,
        grid=(t // tm,),
        in_specs=[pl.BlockSpec((tm, d), lambda i: (i, 0)),
                  pl.BlockSpec((e, d), lambda i: (0, 0)),
                  pl.BlockSpec((e, 1), lambda i: (0, 0))],
        out_specs=[tok, tok, tok, pl.BlockSpec((e, LANES), lambda i: (0, 0))],
        out_shape=[jax.ShapeDtypeStruct((MOE_TOPK, t), jnp.int32),
                   jax.ShapeDtypeStruct((MOE_TOPK, t), jnp.float32),
                   jax.ShapeDtypeStruct((MOE_TOPK, t), jnp.int32),
                   jax.ShapeDtypeStruct((e, LANES), jnp.float32)],
        scratch_shapes=[pltpu.VMEM((e, LANES), jnp.float32)],
        compiler_params=_cparams("arbitrary"),
        name="router_topk",
    )(x1, router_w.T, router_bias.reshape(e, 1).astype(jnp.float32))


def _dispatch_kernel(dest_ref, xp_ref, xs_in, xs_out, sem, *, tm):
    del xs_in
    ps = PACK_SUBLANES

    def copy(j):
        tok = lax.rem(j, tm)
        row = dest_ref[0, 0, j]
        return pltpu.make_async_copy(xp_ref.at[pl.ds(pl.multiple_of(tok * ps, ps), ps), :],
                                     xs_out.at[pl.ds(pl.multiple_of(row * ps, ps), ps), :], sem)

    def body(j, carry):
        copy(j).start()
        return carry

    lax.fori_loop(0, MOE_TOPK * tm, body, 0, unroll=8)
    for _ in range(MOE_TOPK):
        pltpu.make_async_copy(xp_ref, xs_out.at[pl.ds(0, tm * ps), :], sem).wait()


def _dispatch(x1p, dest_kt, n_rows, *, tm=512):
    k, t = dest_kt.shape
    ps = PACK_SUBLANES
    nsteps = t // tm
    dest3 = dest_kt.reshape(k, nsteps, tm).transpose(1, 0, 2).reshape(nsteps, 1, k * tm)
    zeros = jnp.zeros((n_rows * ps, LANES), jnp.int32)
    return pl.pallas_call(
        functools.partial(_dispatch_kernel, tm=tm),
        grid=(nsteps,),
        in_specs=[pl.BlockSpec((1, 1, k * tm), lambda i: (i, 0, 0), memory_space=pltpu.SMEM),
                  pl.BlockSpec((tm * ps, LANES), lambda i: (i, 0)),
                  pl.BlockSpec(memory_space=pl.ANY)],
        out_specs=pl.BlockSpec(memory_space=pl.ANY),
        out_shape=jax.ShapeDtypeStruct((n_rows * ps, LANES), jnp.int32),
        scratch_shapes=[pltpu.SemaphoreType.DMA(())],
        input_output_aliases={2: 0},
        compiler_params=_cparams("arbitrary"),
        name="dispatch_rows",
    )(dest3, x1p, zeros)


def _expert_kernel(blk_e_ref, nvalid_ref, xs_ref, wgu_ref, wd_ref, y_ref):
    del blk_e_ref
    i = pl.program_id(0)
    rows = MOE_ROWS

    @pl.when(i < nvalid_ref[0])
    def _():
        xb = _unpack_rows(xs_ref, rows)
        h = jnp.dot(xb, wgu_ref[...], preferred_element_type=jnp.float32)
        act = _silu(h[:, :MOE_FFN]) * h[:, MOE_FFN:]
        y = jnp.dot(act.astype(jnp.bfloat16), wd_ref[...], preferred_element_type=jnp.float32)
        for c in range(ROW_SUBLANES):
            y_ref[pl.ds(c, rows, stride=ROW_SUBLANES), :] = y[:, c * LANES:(c + 1) * LANES]

    @pl.when(i >= nvalid_ref[0])
    def _():
        y_ref[...] = jnp.zeros_like(y_ref)


def _routed_experts(xs, blk_expert, nvalid, wgu, wd):
    nb = blk_expert.shape[0]
    rows, d = MOE_ROWS, D_MODEL
    grid_spec = pltpu.PrefetchScalarGridSpec(
        num_scalar_prefetch=2,
        grid=(nb,),
        in_specs=[pl.BlockSpec((rows * PACK_SUBLANES, LANES), lambda i, be, nv: (i, 0)),
                  pl.BlockSpec((None, d, 2 * MOE_FFN), lambda i, be, nv: (be[i], 0, 0)),
                  pl.BlockSpec((None, MOE_FFN, d), lambda i, be, nv: (be[i], 0, 0))],
        out_specs=pl.BlockSpec((rows * ROW_SUBLANES, LANES), lambda i, be, nv: (i, 0)),
    )
    return pl.pallas_call(
        _expert_kernel,
        grid_spec=grid_spec,
        out_shape=jax.ShapeDtypeStruct((nb * rows * ROW_SUBLANES, LANES), jnp.float32),
        compiler_params=_cparams("parallel"),
        name="routed_experts",
    )(blk_expert, nvalid, xs, wgu, wd)


def _shared_kernel(x_ref, wgu_ref, wd_ref, o_ref):
    xb = x_ref[...].astype(jnp.bfloat16)
    h = jnp.dot(xb, wgu_ref[...], preferred_element_type=jnp.float32)
    act = _silu(h[:, :MOE_FFN]) * h[:, MOE_FFN:]
    o_ref[...] = jnp.dot(act.astype(jnp.bfloat16), wd_ref[...], preferred_element_type=jnp.float32)


def _shared_expert(x1, wgu, wd, *, tm=512):
    t, d = x1.shape
    return pl.pallas_call(
        _shared_kernel,
        grid=(t // tm,),
        in_specs=[pl.BlockSpec((tm, d), lambda i: (i, 0)),
                  pl.BlockSpec((d, 2 * MOE_FFN), lambda i: (0, 0)),
                  pl.BlockSpec((MOE_FFN, d), lambda i: (0, 0))],
        out_specs=pl.BlockSpec((tm, d), lambda i: (i, 0)),
        out_shape=jax.ShapeDtypeStruct((t, d), jnp.float32),
        compiler_params=_cparams("parallel"),
        name="shared_expert",
    )(x1, wgu, wd)


def _combine_kernel(dest_ref, dest_next_ref, ys_hbm, w_ref, x_ref, sh_ref, g_ref, b_ref, o_ref, ob_ref,
                    ybuf, acc, sem, *, alpha, nsteps):
    i = pl.program_id(0)
    tc, rs = COMBINE_TOK, ROW_SUBLANES
    n_rows = tc * MOE_TOPK

    def start_block(d_ref, dst_slot):
        def body(r, carry):
            row = d_ref[0, 0, r]
            pltpu.make_async_copy(ys_hbm.at[pl.ds(pl.multiple_of(row * rs, rs), rs), :],
                                  ybuf.at[dst_slot, pl.ds(pl.multiple_of(r * rs, rs), rs), :],
                                  sem.at[dst_slot]).start()
            return carry
        lax.fori_loop(0, n_rows, body, 0, unroll=8)

    @pl.when(i == 0)
    def _():
        start_block(dest_ref, 0)

    for slot in range(2):
        @pl.when(lax.rem(i, 2) == slot)
        def _(slot=slot):
            @pl.when(i + 1 < nsteps)
            def _():
                start_block(dest_next_ref, 1 - slot)

            pltpu.make_async_copy(ys_hbm.at[pl.ds(0, n_rows * rs), :], ybuf.at[slot], sem.at[slot]).wait()
            for c in range(rs):
                cols = slice(c * LANES, (c + 1) * LANES)
                a = alpha * x_ref[:, cols] + sh_ref[:, cols]
                for k in range(MOE_TOPK):
                    a = a + w_ref[:, k:k + 1] * ybuf[slot, pl.ds(k * tc * rs + c, tc, stride=rs), :]
                acc[:, cols] = a

    x2 = _layer_norm_rows(acc[...], g_ref[...], b_ref[...])
    o_ref[...] = x2
    ob_ref[...] = x2.astype(ob_ref.dtype)


def _combine_ln(ys, dest_kt, wts_tk, x1, shared, g, b, *, alpha):
    t, d = x1.shape
    tc, rs = COMBINE_TOK, ROW_SUBLANES
    nsteps = t // tc
    dest3 = dest_kt.reshape(MOE_TOPK, nsteps, tc).transpose(1, 0, 2).reshape(nsteps, 1, MOE_TOPK * tc)
    smem_blk = lambda f: pl.BlockSpec((1, 1, MOE_TOPK * tc), f, memory_space=pltpu.SMEM)
    row = lambda i: (i, 0)
    const = lambda i: (0, 0)
    return pl.pallas_call(
        functools.partial(_combine_kernel, alpha=alpha, nsteps=nsteps),
        grid=(nsteps,),
        in_specs=[smem_blk(lambda i: (i, 0, 0)),
                  smem_blk(lambda i: (jnp.minimum(i + 1, nsteps - 1), 0, 0)),
                  pl.BlockSpec(memory_space=pl.ANY),
                  pl.BlockSpec((tc, MOE_TOPK), row),
                  pl.BlockSpec((tc, d), row), pl.BlockSpec((tc, d), row),
                  pl.BlockSpec((1, d), const), pl.BlockSpec((1, d), const)],
        out_specs=[pl.BlockSpec((tc, d), row), pl.BlockSpec((tc, d), row)],
        out_shape=[jax.ShapeDtypeStruct((t, d), jnp.float32), jax.ShapeDtypeStruct((t, d), jnp.bfloat16)],
        scratch_shapes=[pltpu.VMEM((2, MOE_TOPK * tc * rs, LANES), jnp.float32),
                        pltpu.VMEM((tc, d), jnp.float32),
                        pltpu.SemaphoreType.DMA((2,))],
        compiler_params=_cparams("arbitrary"),
        name="combine_ln",
    )(dest3, dest3, ys, wts_tk, x1, shared, g.reshape(1, d), b.reshape(1, d))


def _dispatch_plan(idx_kt, rank_kt, counts):
    k, t = idx_kt.shape
    e, rows = MOE_EXPERTS, MOE_ROWS
    nb = k * t // rows + e
    padded = (counts + rows - 1) // rows * rows
    pend = jnp.cumsum(padded)
    pstart = pend - padded
    blk_start = jnp.arange(nb, dtype=jnp.int32) * rows
    blk_expert = jnp.minimum(jnp.sum(pend[None, :] <= blk_start[:, None], axis=1), e - 1).astype(jnp.int32)
    nvalid = (pend[-1] // rows).astype(jnp.int32).reshape(1)
    onehot = idx_kt[:, :, None] == jnp.arange(e, dtype=jnp.int32)
    dest = rank_kt + jnp.sum(jnp.where(onehot, pstart, 0), axis=-1).astype(jnp.int32)
    return blk_expert, nvalid, dest, nb


def kernel(x, w_in, conv_a_w, conv_a_b, ln_a_g, ln_a_b, conv_c_w, w_branch, w_out, ln1_g, ln1_b, router_w,
           router_bias, w_gate, w_up, w_down, ws_gate, ws_up, ws_down, ln2_g, ln2_b):
    b, s, d = x.shape
    depth = w_in.shape[0]
    t = b * s
    alpha = (DEPTH_ALPHA_BASE * depth) ** 0.25
    bf16 = jnp.bfloat16
    xf = x.reshape(t, d)
    xb = xf.astype(bf16)
    for l in range(depth):
        u = _matmul(xb, w_in[l].astype(bf16), tm=min(t, 1024), tn=1024, out_dtype=bf16)
        u3 = u.reshape(b, s, N_IN)
        a, c = _conv_branches(u3, conv_a_w[l], conv_a_b[l], ln_a_g[l], ln_a_b[l], conv_c_w[l])
        att = _attention(u3)
        merged = _branch_merge(a.reshape(t, BRANCH_W), att.reshape(t, BRANCH_W), c.reshape(t, BRANCH_W), u,
                               w_branch[l].astype(bf16))
        x1, x1p = _outproj_ln(merged, w_out[l].astype(bf16), xf, ln1_g[l], ln1_b[l], alpha=alpha)
        idx_kt, wts_kt, rank_kt, counts = _router(x1, router_w[l], router_bias[l])
        blk_expert, nvalid, dest, nb = _dispatch_plan(idx_kt, rank_kt, counts[:, 0].astype(jnp.int32))
        xs = _dispatch(x1p, dest, nb * MOE_ROWS)
        wgu = jnp.concatenate([w_gate[l], w_up[l]], axis=-1).astype(bf16)
        ys = _routed_experts(xs, blk_expert, nvalid, wgu, w_down[l].astype(bf16))
        wsgu = jnp.concatenate([ws_gate[l], ws_up[l]], axis=-1).astype(bf16)
        shared = _shared_expert(x1, wsgu, ws_down[l].astype(bf16))
        xf, xb = _combine_ln(ys, dest, wts_kt.T, x1, shared, ln2_g[l], ln2_b[l], alpha=alpha)
    return xf.reshape(b, s, d)
```

```python
import functools

import jax
import jax.numpy as jnp
from jax import lax
from jax.experimental import pallas as pl
from jax.experimental.pallas import tpu as pltpu

D_MODEL = 2048
BRANCH_W = D_MODEL // 2
N_BRANCHES = 3
CONV_A_WIDTH = 31
CONV_C_WIDTH = 3
ATT_DILATIONS = (1, 4, 16)
ATT_BLOCK = 128
ATT_HEADS = 8
ATT_HEAD_DIM = BRANCH_W // ATT_HEADS
ATT_W = len(ATT_DILATIONS) * BRANCH_W
OFF_Q = 2 * BRANCH_W
OFF_K = OFF_Q + ATT_W
OFF_V = OFF_K + ATT_W
OFF_C = OFF_V + ATT_W
OFF_G = OFF_C + 3 * BRANCH_W
N_IN = OFF_G + N_BRANCHES * D_MODEL
MOE_EXPERTS = 64
MOE_TOPK = 8
MOE_GROUPS = 8
MOE_GROUP_SIZE = MOE_EXPERTS // MOE_GROUPS
MOE_TOPK_GROUPS = 4
MOE_FFN = D_MODEL // 8
MOE_ROUTED_SCALE = 2.5
LN_EPS = 1e-5
NEG_INF = -1e30
DEPTH_ALPHA_BASE = 2.0

LANES = 128
VMEM_LIMIT_BYTES = 52 * 1024 * 1024
MOE_ROWS = 256
PACK_SUBLANES = D_MODEL // 2 // LANES
SUBLANES = 8
HIGH_HALF_MASK = -65536
CONV_TS = 256
CONV_HALO = 32
CONV_CHUNK = 32
COMBINE_TOK = 128
DMA_PRIORITIES = 2


def _cparams(*sem):
    return pltpu.CompilerParams(dimension_semantics=sem, vmem_limit_bytes=VMEM_LIMIT_BYTES)


def _sigmoid(v):
    return 1.0 / (1.0 + jnp.exp(-v))


def _silu(v):
    return v * _sigmoid(v)


def _layer_norm_rows(v, g, b):
    mu = jnp.mean(v, axis=-1, keepdims=True)
    vc = v - mu
    var = jnp.mean(vc * vc, axis=-1, keepdims=True)
    return vc * lax.rsqrt(var + LN_EPS) * g + b


def _matmul_kernel(x_ref, w_ref, o_ref):
    o_ref[...] = jnp.dot(x_ref[...], w_ref[...], preferred_element_type=jnp.float32).astype(o_ref.dtype)


def _matmul(x, w, *, tm, tn, out_dtype):
    m, k = x.shape
    _, n = w.shape
    return pl.pallas_call(
        _matmul_kernel,
        grid=(m // tm, n // tn),
        in_specs=[pl.BlockSpec((tm, k), lambda i, j: (i, 0)),
                  pl.BlockSpec((k, tn), lambda i, j: (0, j))],
        out_specs=pl.BlockSpec((tm, tn), lambda i, j: (i, j)),
        out_shape=jax.ShapeDtypeStruct((m, n), out_dtype),
        compiler_params=_cparams("parallel", "parallel"),
        name="in_proj",
    )(x, w)


def _conv_kernel(val_ref, gate_ref, hval_ref, hgate_ref, cb_ref, cc_ref, ch_ref, hcc_ref, hch_ref,
                 wa_ref, ba_ref, ga_ref, bta_ref, wc_ref, a_ref, c_ref, abuf, cbuf, shifted):
    first = pl.program_id(1) == 0
    f32 = jnp.float32
    glu_h = hval_ref[0].astype(f32) * _sigmoid(hgate_ref[0].astype(f32))
    abuf[0:CONV_HALO, :] = jnp.where(first, 0.0, glu_h)
    abuf[CONV_HALO:, :] = val_ref[0].astype(f32) * _sigmoid(gate_ref[0].astype(f32))
    prod_h = hcc_ref[0].astype(f32) * hch_ref[0].astype(f32)
    cbuf[0:CONV_HALO, :] = jnp.where(first, 0.0, prod_h)
    cbuf[CONV_HALO:, :] = cc_ref[0].astype(f32) * ch_ref[0].astype(f32)

    span = CONV_HALO + CONV_TS - SUBLANES
    for j in range(1, SUBLANES):
        shifted[j - 1, 0:span, :] = abuf[j:j + span, :]

    a_off = CONV_HALO - (CONV_A_WIDTH - 1)
    c_off = CONV_HALO - (CONV_C_WIDTH - 1)
    groups = CONV_CHUNK // SUBLANES
    for r0 in range(0, CONV_TS, CONV_CHUNK):
        acc = jnp.zeros((groups, SUBLANES, BRANCH_W), f32)
        for k in range(CONV_A_WIDTH):
            j = (a_off + k) % SUBLANES
            base = r0 + a_off + k - j
            rows = abuf[base:base + CONV_CHUNK, :] if j == 0 else shifted[j - 1, base:base + CONV_CHUNK, :]
            tap = wa_ref[k * SUBLANES:(k + 1) * SUBLANES, :]
            acc = acc + tap[None] * rows.reshape(groups, SUBLANES, BRANCH_W)
        acc = acc.reshape(CONV_CHUNK, BRANCH_W)
        y = _layer_norm_rows(acc + ba_ref[...], ga_ref[...], bta_ref[...])
        a_ref[0, r0:r0 + CONV_CHUNK, :] = _silu(y).astype(a_ref.dtype)
        cacc = jnp.zeros((CONV_CHUNK, BRANCH_W), f32)
        for k in range(CONV_C_WIDTH):
            cacc = cacc + wc_ref[k:k + 1, :] * cbuf[r0 + c_off + k:r0 + c_off + k + CONV_CHUNK, :]
        c_ref[0, r0:r0 + CONV_CHUNK, :] = (cb_ref[0, r0:r0 + CONV_CHUNK, :].astype(f32) * cacc).astype(c_ref.dtype)


def _conv_branches(u, conv_a_w, conv_a_b, ln_a_g, ln_a_b, conv_c_w):
    b, s, _ = u.shape
    ts, halo = CONV_TS, CONV_HALO
    w = BRANCH_W
    ratio = ts // halo

    def main(col):
        return pl.BlockSpec((1, ts, w), lambda bi, i: (bi, i, col))

    def prev(col):
        return pl.BlockSpec((1, halo, w), lambda bi, i: (bi, jnp.maximum(i * ratio - 1, 0), col))

    def full(rows):
        return pl.BlockSpec((rows, w), lambda bi, i: (0, 0))

    cblk = OFF_C // w
    out = pl.pallas_call(
        _conv_kernel,
        grid=(b, s // ts),
        in_specs=[main(0), main(1), prev(0), prev(1),
                  main(cblk), main(cblk + 1), main(cblk + 2), prev(cblk + 1), prev(cblk + 2),
                  full(CONV_A_WIDTH * SUBLANES), full(1), full(1), full(1), full(CONV_C_WIDTH)],
        out_specs=[pl.BlockSpec((1, ts, w), lambda bi, i: (bi, i, 0)),
                   pl.BlockSpec((1, ts, w), lambda bi, i: (bi, i, 0))],
        out_shape=[jax.ShapeDtypeStruct((b, s, w), jnp.bfloat16),
                   jax.ShapeDtypeStruct((b, s, w), jnp.bfloat16)],
        scratch_shapes=[pltpu.VMEM((halo + ts, w), jnp.float32),
                        pltpu.VMEM((halo + ts, w), jnp.float32),
                        pltpu.VMEM((SUBLANES - 1, halo + ts, w), jnp.float32)],
        compiler_params=_cparams("parallel", "parallel"),
        name="conv_branches",
    )(u, u, u, u, u, u, u, u, u,
      jnp.repeat(conv_a_w, SUBLANES, axis=0), conv_a_b.reshape(1, w), ln_a_g.reshape(1, w), ln_a_b.reshape(1, w),
      conv_c_w)
    return out


def _attn_kernel(q0, q1, q2, k0, k1, k2, v0, v1, v2, o_ref, q32, k32, v32, nbuf, mbuf, zbuf, *, seq):
    f32, bf16 = jnp.float32, jnp.bfloat16
    blk = ATT_BLOCK
    scale = ATT_HEAD_DIM ** -0.5
    qi = lax.broadcasted_iota(jnp.int32, (blk, 2 * blk), 0)
    ki = lax.broadcasted_iota(jnp.int32, (blk, 2 * blk), 1)
    dist = blk + qi - ki
    valid2 = (dist >= 0) & (dist <= blk)
    valid1 = (lax.broadcasted_iota(jnp.int32, (blk, blk), 0)
              >= lax.broadcasted_iota(jnp.int32, (blk, blk), 1))

    for g, (r, qr, kr, vr) in enumerate(zip(ATT_DILATIONS, (q0, q1, q2), (k0, k1, k2), (v0, v1, v2))):
        length = seq // r
        nblk = length // blk
        if r > 1:
            q32[...] = qr[0].astype(f32)
            k32[...] = kr[0].astype(f32)
            v32[...] = vr[0].astype(f32)
        for c in range(r):
            for n in range(nblk):
                q_start = c + n * blk * r
                k_start = c + (n - 1) * blk * r if n > 0 else q_start
                k_len = 2 * blk if n > 0 else blk
                if r == 1:
                    q = qr[0, q_start:q_start + blk, :]
                    k = kr[0, k_start:k_start + k_len, :]
                    v = vr[0, k_start:k_start + k_len, :]
                else:
                    q = q32[pl.ds(q_start, blk, stride=r), :].astype(bf16)
                    k = k32[pl.ds(k_start, k_len, stride=r), :].astype(bf16)
                    v = v32[pl.ds(k_start, k_len, stride=r), :].astype(bf16)
                s = lax.dot_general(q, k, (((1,), (1,)), ((), ())), preferred_element_type=f32) * scale
                s = jnp.where(valid2 if n > 0 else valid1, s, NEG_INF)
                m = jnp.max(s, axis=-1, keepdims=True)
                p = jnp.exp(s - m)
                z = jnp.sum(p, axis=-1, keepdims=True)
                num = jnp.dot(p.astype(bf16), v, preferred_element_type=f32)
                if r == 1:
                    rows = pl.ds(q_start, blk)
                else:
                    rows = pl.ds(q_start, blk, stride=r)
                nbuf[g, rows, :] = num
                mbuf[g, rows, :] = jnp.broadcast_to(m, (blk, ATT_HEAD_DIM))
                zbuf[g, rows, :] = jnp.broadcast_to(z, (blk, ATT_HEAD_DIM))

    rows_per = 256
    for r0 in range(0, seq, rows_per):
        sl = slice(r0, r0 + rows_per)
        m_all = jnp.maximum(jnp.maximum(mbuf[0, sl, :], mbuf[1, sl, :]), mbuf[2, sl, :])
        num = jnp.zeros((rows_per, ATT_HEAD_DIM), f32)
        den = jnp.zeros((rows_per, ATT_HEAD_DIM), f32)
        for g in range(len(ATT_DILATIONS)):
            e = jnp.exp(mbuf[g, sl, :] - m_all)
            num = num + e * nbuf[g, sl, :]
            den = den + e * zbuf[g, sl, :]
        o_ref[0, sl, :] = (num / den).astype(o_ref.dtype)


def _attention(u):
    b, s, _ = u.shape
    dh = ATT_HEAD_DIM
    n_groups = len(ATT_DILATIONS)

    def spec(off, g):
        base = (off + g * BRANCH_W) // dh
        return pl.BlockSpec((1, s, dh), lambda bi, h: (bi, 0, base + h))

    in_specs = ([spec(OFF_Q, g) for g in range(n_groups)] + [spec(OFF_K, g) for g in range(n_groups)]
                + [spec(OFF_V, g) for g in range(n_groups)])
    return pl.pallas_call(
        functools.partial(_attn_kernel, seq=s),
        grid=(b, ATT_HEADS),
        in_specs=in_specs,
        out_specs=pl.BlockSpec((1, s, dh), lambda bi, h: (bi, 0, h)),
        out_shape=jax.ShapeDtypeStruct((b, s, BRANCH_W), jnp.bfloat16),
        scratch_shapes=[pltpu.VMEM((s, dh), jnp.float32)] * 3
        + [pltpu.VMEM((n_groups, s, dh), jnp.float32)] * 3,
        compiler_params=_cparams("parallel", "parallel"),
        name="dilated_attention",
    )(*([u] * 9))


def _merge_kernel(a_ref, att_ref, c_ref, g0_ref, g1_ref, g2_ref, wb_ref, o_ref):
    f32 = jnp.float32
    acc = None
    for n, (br, gr) in enumerate(((a_ref, g0_ref), (att_ref, g1_ref), (c_ref, g2_ref))):
        proj = jnp.dot(br[...], wb_ref[n], preferred_element_type=f32)
        term = _sigmoid(gr[...].astype(f32)) * proj
        acc = term if acc is None else acc + term
    o_ref[...] = acc.astype(o_ref.dtype)


def _branch_merge(a, att, c, u2d, w_branch, *, tm=512, tn=1024):
    t, w = a.shape
    d = D_MODEL
    gblk = OFF_G // tn

    def gate(n):
        return pl.BlockSpec((tm, tn), lambda j, i: (i, gblk + n * (d // tn) + j))

    row = pl.BlockSpec((tm, w), lambda j, i: (i, 0))
    return pl.pallas_call(
        _merge_kernel,
        grid=(d // tn, t // tm),
        in_specs=[row, row, row, gate(0), gate(1), gate(2),
                  pl.BlockSpec((N_BRANCHES, w, tn), lambda j, i: (0, 0, j))],
        out_specs=pl.BlockSpec((tm, tn), lambda j, i: (i, j)),
        out_shape=jax.ShapeDtypeStruct((t, d), jnp.bfloat16),
        compiler_params=_cparams("parallel", "parallel"),
        name="branch_merge",
    )(a, att, c, u2d, u2d, u2d, w_branch)


def _pack_rows(y, out_ref, rows):
    half = D_MODEL // 2
    lo = lax.bitcast_convert_type(y[:, :half].astype(jnp.bfloat16).astype(jnp.float32), jnp.int32)
    hi = lax.bitcast_convert_type(y[:, half:].astype(jnp.bfloat16).astype(jnp.float32), jnp.int32)
    word = (hi & HIGH_HALF_MASK) | lax.shift_right_logical(lo, 16)
    for c in range(PACK_SUBLANES):
        out_ref[pl.ds(c, rows, stride=PACK_SUBLANES), :] = word[:, c * LANES:(c + 1) * LANES]


def _unpack_rows(in_ref, rows):
    los, his = [], []
    for c in range(PACK_SUBLANES):
        word = in_ref[pl.ds(c, rows, stride=PACK_SUBLANES), :]
        los.append(lax.bitcast_convert_type(lax.shift_left(word, 16), jnp.float32).astype(jnp.bfloat16))
        his.append(lax.bitcast_convert_type(word & HIGH_HALF_MASK, jnp.float32).astype(jnp.bfloat16))
    return jnp.concatenate(los + his, axis=1)


def _outproj_ln_kernel(m_ref, w_ref, x_ref, g_ref, b_ref, o_ref, p_ref, *, alpha):
    y = jnp.dot(m_ref[...], w_ref[...], preferred_element_type=jnp.float32)
    x1 = _layer_norm_rows(alpha * x_ref[...] + y, g_ref[...], b_ref[...])
    o_ref[...] = x1
    _pack_rows(x1, p_ref, x1.shape[0])


def _outproj_ln(merged, w_out, x, g, b, *, alpha, tm=512):
    t, d = x.shape
    row = lambda i: (i, 0)
    const = lambda i: (0, 0)
    return pl.pallas_call(
        functools.partial(_outproj_ln_kernel, alpha=alpha),
        grid=(t // tm,),
        in_specs=[pl.BlockSpec((tm, d), row), pl.BlockSpec((d, d), const), pl.BlockSpec((tm, d), row),
                  pl.BlockSpec((1, d), const), pl.BlockSpec((1, d), const)],
        out_specs=[pl.BlockSpec((tm, d), row), pl.BlockSpec((tm * PACK_SUBLANES, LANES), row)],
        out_shape=[jax.ShapeDtypeStruct((t, d), jnp.float32),
                   jax.ShapeDtypeStruct((t * PACK_SUBLANES, LANES), jnp.int32)],
        compiler_params=_cparams("parallel"),
        name="out_proj_ln",
    )(merged, w_out, x, g.reshape(1, d), b.reshape(1, d))


def _first_argmax(vals, iota, axis, size):
    mx = jnp.max(vals, axis=axis, keepdims=True)
    idx = jnp.min(jnp.where(vals == mx, iota, size), axis=axis, keepdims=True)
    return mx, idx


def _router_kernel(x_ref, wt_ref, bias_ref, idx_ref, wts_ref, rank_ref, cnt_ref, carry):
    f32 = jnp.float32
    e, gsz, ng = MOE_EXPERTS, MOE_GROUP_SIZE, MOE_GROUPS
    tm = x_ref.shape[0]

    @pl.when(pl.program_id(0) == 0)
    def _():
        carry[...] = jnp.zeros_like(carry)

    logits = lax.dot_general(wt_ref[...], x_ref[...], (((1,), (1,)), ((), ())),
                             precision=lax.Precision.HIGHEST, preferred_element_type=f32)
    scores = _sigmoid(logits)
    choice = scores + bias_ref[...]
    grp = choice.reshape(ng, gsz, tm)
    iota_in = lax.broadcasted_iota(jnp.int32, (ng, gsz, tm), 1)
    m1, i1 = _first_argmax(grp, iota_in, 1, gsz)
    m2 = jnp.max(jnp.where(iota_in == i1, -jnp.inf, grp), axis=1, keepdims=True)
    grp_score = (m1 + m2).reshape(ng, tm)
    iota_g = lax.broadcasted_iota(jnp.int32, (ng, tm), 0)
    grp_mask = jnp.zeros((ng, tm), jnp.bool_)
    for _ in range(MOE_TOPK_GROUPS):
        _, gi = _first_argmax(grp_score, iota_g, 0, ng)
        hit = iota_g == gi
        grp_mask = grp_mask | hit
        grp_score = jnp.where(hit, -jnp.inf, grp_score)
    expert_mask = jnp.broadcast_to(grp_mask.reshape(ng, 1, tm), (ng, gsz, tm)).reshape(e, tm)
    masked = jnp.where(expert_mask, choice, NEG_INF)
    iota_e = lax.broadcasted_iota(jnp.int32, (e, tm), 0)
    idxs, ws, hits = [], [], []
    for _ in range(MOE_TOPK):
        _, ei = _first_argmax(masked, iota_e, 0, e)
        hit = iota_e == ei
        idxs.append(ei)
        hits.append(hit)
        ws.append(jnp.sum(jnp.where(hit, scores, 0.0), axis=0, keepdims=True))
        masked = jnp.where(hit, -jnp.inf, masked)
    w_all = jnp.concatenate(ws, axis=0)
    idx_ref[...] = jnp.concatenate(idxs, axis=0)
    wts_ref[...] = w_all / jnp.sum(w_all, axis=0, keepdims=True) * MOE_ROUTED_SCALE

    chosen = hits[0]
    for hit in hits[1:]:
        chosen = chosen | hit
    chosen_f = jnp.where(chosen, 1.0, 0.0)
    earlier = (lax.broadcasted_iota(jnp.int32, (tm, tm), 0) < lax.broadcasted_iota(jnp.int32, (tm, tm), 1))
    prefix = jnp.dot(chosen_f.astype(jnp.bfloat16), jnp.where(earlier, 1.0, 0.0).astype(jnp.bfloat16),
                     preferred_element_type=f32)
    base = carry[:, 0:1] + prefix
    ranks = [jnp.sum(jnp.where(hit, base, 0.0), axis=0, keepdims=True) for hit in hits]
    rank_ref[...] = jnp.concatenate(ranks, axis=0).astype(jnp.int32)
    carry[...] = carry[...] + jnp.sum(chosen_f, axis=1, keepdims=True)
    cnt_ref[...] = carry[...]


def _router(x1, router_w, router_bias, *, tm=512):
    t, d = x1.shape
    e = MOE_EXPERTS
    tok = pl.BlockSpec((MOE_TOPK, tm), lambda i: (0, i))
    return pl.pallas_call(
        _router_kernel,
        grid=(t // tm,),
        in_specs=[pl.BlockSpec((tm, d), lambda i: (i, 0)),
                  pl.BlockSpec((e, d), lambda i: (0, 0)),
                  pl.BlockSpec((e, 1), lambda i: (0, 0))],
        out_specs=[tok, tok, tok, pl.BlockSpec((e, LANES), lambda i: (0, 0))],
        out_shape=[jax.ShapeDtypeStruct((MOE_TOPK, t), jnp.int32),
                   jax.ShapeDtypeStruct((MOE_TOPK, t), jnp.float32),
                   jax.ShapeDtypeStruct((MOE_TOPK, t), jnp.int32),
                   jax.ShapeDtypeStruct((e, LANES), jnp.float32)],
        scratch_shapes=[pltpu.VMEM((e, LANES), jnp.float32)],
        compiler_params=_cparams("arbitrary"),
        name="router_topk",
    )(x1, router_w.T, router_bias.reshape(e, 1).astype(jnp.float32))


def _dispatch_kernel(dest_ref, xp_ref, xs_in, xs_out, sem, *, tm):
    del xs_in
    ps = PACK_SUBLANES

    for k in range(MOE_TOPK):
        def body(pair, carry, k=k):
            for prio in range(DMA_PRIORITIES):
                tok = pair * DMA_PRIORITIES + prio
                row = dest_ref[0, 0, k * tm + tok]
                pltpu.make_async_copy(xp_ref.at[pl.ds(pl.multiple_of(tok * ps, ps), ps), :],
                                      xs_out.at[pl.ds(pl.multiple_of(row * ps, ps), ps), :],
                                      sem).start(priority=prio)
            return carry

        lax.fori_loop(0, tm // DMA_PRIORITIES, body, 0, unroll=4)
    for _ in range(MOE_TOPK):
        pltpu.make_async_copy(xp_ref, xs_out.at[pl.ds(0, tm * ps), :], sem).wait()


def _dispatch(x1p, dest_kt, n_rows, *, tm=512):
    k, t = dest_kt.shape
    ps = PACK_SUBLANES
    nsteps = t // tm
    dest3 = dest_kt.reshape(k, nsteps, tm).transpose(1, 0, 2).reshape(nsteps, 1, k * tm)
    zeros = jnp.zeros((n_rows * ps, LANES), jnp.int32)
    return pl.pallas_call(
        functools.partial(_dispatch_kernel, tm=tm),
        grid=(nsteps,),
        in_specs=[pl.BlockSpec((1, 1, k * tm), lambda i: (i, 0, 0), memory_space=pltpu.SMEM),
                  pl.BlockSpec((tm * ps, LANES), lambda i: (i, 0)),
                  pl.BlockSpec(memory_space=pl.ANY)],
        out_specs=pl.BlockSpec(memory_space=pl.ANY),
        out_shape=jax.ShapeDtypeStruct((n_rows * ps, LANES), jnp.int32),
        scratch_shapes=[pltpu.SemaphoreType.DMA(())],
        input_output_aliases={2: 0},
        compiler_params=_cparams("arbitrary"),
        name="dispatch_rows",
    )(dest3, x1p, zeros)


def _expert_kernel(blk_e_ref, nvalid_ref, xs_ref, wgu_ref, wd_ref, y_ref):
    del blk_e_ref
    i = pl.program_id(0)
    rows = MOE_ROWS

    @pl.when(i < nvalid_ref[0])
    def _():
        xb = _unpack_rows(xs_ref, rows)
        h = jnp.dot(xb, wgu_ref[...], preferred_element_type=jnp.float32)
        act = _silu(h[:, :MOE_FFN]) * h[:, MOE_FFN:]
        y = jnp.dot(act.astype(jnp.bfloat16), wd_ref[...], preferred_element_type=jnp.float32)
        _pack_rows(y, y_ref, rows)

    @pl.when(i >= nvalid_ref[0])
    def _():
        y_ref[...] = jnp.zeros_like(y_ref)


def _routed_experts(xs, blk_expert, nvalid, wgu, wd):
    nb = blk_expert.shape[0]
    rows, d = MOE_ROWS, D_MODEL
    grid_spec = pltpu.PrefetchScalarGridSpec(
        num_scalar_prefetch=2,
        grid=(nb,),
        in_specs=[pl.BlockSpec((rows * PACK_SUBLANES, LANES), lambda i, be, nv: (i, 0)),
                  pl.BlockSpec((None, d, 2 * MOE_FFN), lambda i, be, nv: (be[i], 0, 0)),
                  pl.BlockSpec((None, MOE_FFN, d), lambda i, be, nv: (be[i], 0, 0))],
        out_specs=pl.BlockSpec((rows * PACK_SUBLANES, LANES), lambda i, be, nv: (i, 0)),
    )
    return pl.pallas_call(
        _expert_kernel,
        grid_spec=grid_spec,
        out_shape=jax.ShapeDtypeStruct((nb * rows * PACK_SUBLANES, LANES), jnp.int32),
        compiler_params=_cparams("parallel"),
        name="routed_experts",
    )(blk_expert, nvalid, xs, wgu, wd)


def _shared_kernel(x_ref, wgu_ref, wd_ref, o_ref):
    xb = x_ref[...].astype(jnp.bfloat16)
    h = jnp.dot(xb, wgu_ref[...], preferred_element_type=jnp.float32)
    act = _silu(h[:, :MOE_FFN]) * h[:, MOE_FFN:]
    o_ref[...] = jnp.dot(act.astype(jnp.bfloat16), wd_ref[...], preferred_element_type=jnp.float32)


def _shared_expert(x1, wgu, wd, *, tm=512):
    t, d = x1.shape
    return pl.pallas_call(
        _shared_kernel,
        grid=(t // tm,),
        in_specs=[pl.BlockSpec((tm, d), lambda i: (i, 0)),
                  pl.BlockSpec((d, 2 * MOE_FFN), lambda i: (0, 0)),
                  pl.BlockSpec((MOE_FFN, d), lambda i: (0, 0))],
        out_specs=pl.BlockSpec((tm, d), lambda i: (i, 0)),
        out_shape=jax.ShapeDtypeStruct((t, d), jnp.float32),
        compiler_params=_cparams("parallel"),
        name="shared_expert",
    )(x1, wgu, wd)


def _combine_kernel(dest_ref, dest_next_ref, ys_hbm, w_ref, x_ref, sh_ref, g_ref, b_ref, o_ref, ob_ref,
                    ybuf, acc, sem, *, alpha, nsteps):
    i = pl.program_id(0)
    tc, ps = COMBINE_TOK, PACK_SUBLANES
    n_rows = tc * MOE_TOPK
    half = D_MODEL // 2
    f32 = jnp.float32

    def start_block(d_ref, dst_slot):
        def body(pair, carry):
            for prio in range(DMA_PRIORITIES):
                r = pair * DMA_PRIORITIES + prio
                row = d_ref[0, 0, r]
                pltpu.make_async_copy(ys_hbm.at[pl.ds(pl.multiple_of(row * ps, ps), ps), :],
                                      ybuf.at[dst_slot, pl.ds(pl.multiple_of(r * ps, ps), ps), :],
                                      sem.at[dst_slot]).start(priority=prio)
            return carry
        lax.fori_loop(0, n_rows // DMA_PRIORITIES, body, 0, unroll=4)

    @pl.when(i == 0)
    def _():
        start_block(dest_ref, 0)

    for slot in range(2):
        @pl.when(lax.rem(i, 2) == slot)
        def _(slot=slot):
            @pl.when(i + 1 < nsteps)
            def _():
                start_block(dest_next_ref, 1 - slot)

            pltpu.make_async_copy(ys_hbm.at[pl.ds(0, n_rows * ps), :], ybuf.at[slot], sem.at[slot]).wait()
            for c in range(ps):
                lo_cols = slice(c * LANES, (c + 1) * LANES)
                hi_cols = slice(half + c * LANES, half + (c + 1) * LANES)
                a_lo = alpha * x_ref[:, lo_cols] + sh_ref[:, lo_cols]
                a_hi = alpha * x_ref[:, hi_cols] + sh_ref[:, hi_cols]
                for k in range(MOE_TOPK):
                    word = ybuf[slot, pl.ds(k * tc * ps + c, tc, stride=ps), :]
                    wk = w_ref[:, k:k + 1]
                    a_lo = a_lo + wk * lax.bitcast_convert_type(lax.shift_left(word, 16), f32)
                    a_hi = a_hi + wk * lax.bitcast_convert_type(word & HIGH_HALF_MASK, f32)
                acc[:, lo_cols] = a_lo
                acc[:, hi_cols] = a_hi

    x2 = _layer_norm_rows(acc[...], g_ref[...], b_ref[...])
    o_ref[...] = x2
    ob_ref[...] = x2.astype(ob_ref.dtype)


def _combine_ln(ys, dest_kt, wts_tk, x1, shared, g, b, *, alpha):
    t, d = x1.shape
    tc = COMBINE_TOK
    nsteps = t // tc
    dest3 = dest_kt.reshape(MOE_TOPK, nsteps, tc).transpose(1, 0, 2).reshape(nsteps, 1, MOE_TOPK * tc)
    smem_blk = lambda f: pl.BlockSpec((1, 1, MOE_TOPK * tc), f, memory_space=pltpu.SMEM)
    row = lambda i: (i, 0)
    const = lambda i: (0, 0)
    return pl.pallas_call(
        functools.partial(_combine_kernel, alpha=alpha, nsteps=nsteps),
        grid=(nsteps,),
        in_specs=[smem_blk(lambda i: (i, 0, 0)),
                  smem_blk(lambda i: (jnp.minimum(i + 1, nsteps - 1), 0, 0)),
                  pl.BlockSpec(memory_space=pl.ANY),
                  pl.BlockSpec((tc, MOE_TOPK), row),
                  pl.BlockSpec((tc, d), row), pl.BlockSpec((tc, d), row),
                  pl.BlockSpec((1, d), const), pl.BlockSpec((1, d), const)],
        out_specs=[pl.BlockSpec((tc, d), row), pl.BlockSpec((tc, d), row)],
        out_shape=[jax.ShapeDtypeStruct((t, d), jnp.float32), jax.ShapeDtypeStruct((t, d), jnp.bfloat16)],
        scratch_shapes=[pltpu.VMEM((2, MOE_TOPK * tc * PACK_SUBLANES, LANES), jnp.int32),
                        pltpu.VMEM((tc, d), jnp.float32),
                        pltpu.SemaphoreType.DMA((2,))],
        compiler_params=_cparams("arbitrary"),
        name="combine_ln",
    )(dest3, dest3, ys, wts_tk, x1, shared, g.reshape(1, d), b.reshape(1, d))


def _dispatch_plan(idx_kt, rank_kt, counts):
    k, t = idx_kt.shape
    e, rows = MOE_EXPERTS, MOE_ROWS
    nb = k * t // rows + e
    padded = (counts + rows - 1) // rows * rows
    pend = jnp.cumsum(padded)
    pstart = pend - padded
    blk_start = jnp.arange(nb, dtype=jnp.int32) * rows
    blk_expert = jnp.minimum(jnp.sum(pend[None, :] <= blk_start[:, None], axis=1), e - 1).astype(jnp.int32)
    nvalid = (pend[-1] // rows).astype(jnp.int32).reshape(1)
    onehot = idx_kt[:, :, None] == jnp.arange(e, dtype=jnp.int32)
    dest = rank_kt + jnp.sum(jnp.where(onehot, pstart, 0), axis=-1).astype(jnp.int32)
    return blk_expert, nvalid, dest, nb


def kernel(x, w_in, conv_a_w, conv_a_b, ln_a_g, ln_a_b, conv_c_w, w_branch, w_out, ln1_g, ln1_b, router_w,
           router_bias, w_gate, w_up, w_down, ws_gate, ws_up, ws_down, ln2_g, ln2_b):
    b, s, d = x.shape
    depth = w_in.shape[0]
    t = b * s
    alpha = (DEPTH_ALPHA_BASE * depth) ** 0.25
    bf16 = jnp.bfloat16
    xf = x.reshape(t, d)
    xb = xf.astype(bf16)
    for l in range(depth):
        u = _matmul(xb, w_in[l].astype(bf16), tm=min(t, 1024), tn=1024, out_dtype=bf16)
        u3 = u.reshape(b, s, N_IN)
        a, c = _conv_branches(u3, conv_a_w[l], conv_a_b[l], ln_a_g[l], ln_a_b[l], conv_c_w[l])
        att = _attention(u3)
        merged = _branch_merge(a.reshape(t, BRANCH_W), att.reshape(t, BRANCH_W), c.reshape(t, BRANCH_W), u,
                               w_branch[l].astype(bf16))
        x1, x1p = _outproj_ln(merged, w_out[l].astype(bf16), xf, ln1_g[l], ln1_b[l], alpha=alpha)
        idx_kt, wts_kt, rank_kt, counts = _router(x1, router_w[l], router_bias[l])
        blk_expert, nvalid, dest, nb = _dispatch_plan(idx_kt, rank_kt, counts[:, 0].astype(jnp.int32))
        xs = _dispatch(x1p, dest, nb * MOE_ROWS)
        wgu = jnp.concatenate([w_gate[l], w_up[l]], axis=-1).astype(bf16)
        ys = _routed_experts(xs, blk_expert, nvalid, wgu, w_down[l].astype(bf16))
        wsgu = jnp.concatenate([ws_gate[l], ws_up[l]], axis=-1).astype(bf16)
        shared = _shared_expert(x1, wsgu, ws_down[l].astype(bf16))
        xf, xb = _combine_ln(ys, dest, wts_kt.T, x1, shared, ln2_g[l], ln2_b[l], alpha=alpha)
    return xf.reshape(b, s, d)
```

```python
import functools

import jax
import jax.numpy as jnp
from jax import lax
from jax.experimental import pallas as pl
from jax.experimental.pallas import tpu as pltpu

D_MODEL = 2048
BRANCH_W = D_MODEL // 2
N_BRANCHES = 3
CONV_A_WIDTH = 31
CONV_C_WIDTH = 3
ATT_DILATIONS = (1, 4, 16)
ATT_BLOCK = 128
ATT_HEADS = 8
ATT_HEAD_DIM = BRANCH_W // ATT_HEADS
ATT_W = len(ATT_DILATIONS) * BRANCH_W
OFF_Q = 2 * BRANCH_W
OFF_K = OFF_Q + ATT_W
OFF_V = OFF_K + ATT_W
OFF_C = OFF_V + ATT_W
OFF_G = OFF_C + 3 * BRANCH_W
N_IN = OFF_G + N_BRANCHES * D_MODEL
MOE_EXPERTS = 64
MOE_TOPK = 8
MOE_GROUPS = 8
MOE_GROUP_SIZE = MOE_EXPERTS // MOE_GROUPS
MOE_TOPK_GROUPS = 4
MOE_FFN = D_MODEL // 8
MOE_ROUTED_SCALE = 2.5
LN_EPS = 1e-5
NEG_INF = -1e30
DEPTH_ALPHA_BASE = 2.0

LANES = 128
VMEM_LIMIT_BYTES = 52 * 1024 * 1024
MOE_ROWS = 512
PACK_SUBLANES = D_MODEL // 2 // LANES
SUBLANES = 8
HIGH_HALF_MASK = -65536
CONV_TS = 256
CONV_HALO = 32
CONV_CHUNK = 32
COMBINE_TOK = 128
DMA_PRIORITIES = 2


def _cparams(*sem):
    return pltpu.CompilerParams(dimension_semantics=sem, vmem_limit_bytes=VMEM_LIMIT_BYTES)


def _sigmoid(v):
    return 1.0 / (1.0 + jnp.exp(-v))


def _silu(v):
    return v * _sigmoid(v)


def _layer_norm_rows(v, g, b):
    mu = jnp.mean(v, axis=-1, keepdims=True)
    vc = v - mu
    var = jnp.mean(vc * vc, axis=-1, keepdims=True)
    return vc * lax.rsqrt(var + LN_EPS) * g + b


def _matmul_kernel(x_ref, w_ref, o_ref):
    o_ref[...] = jnp.dot(x_ref[...], w_ref[...], preferred_element_type=jnp.float32).astype(o_ref.dtype)


def _matmul(x, w, *, tm, tn, out_dtype):
    m, k = x.shape
    _, n = w.shape
    return pl.pallas_call(
        _matmul_kernel,
        grid=(m // tm, n // tn),
        in_specs=[pl.BlockSpec((tm, k), lambda i, j: (i, 0)),
                  pl.BlockSpec((k, tn), lambda i, j: (0, j))],
        out_specs=pl.BlockSpec((tm, tn), lambda i, j: (i, j)),
        out_shape=jax.ShapeDtypeStruct((m, n), out_dtype),
        compiler_params=_cparams("parallel", "parallel"),
        name="in_proj",
    )(x, w)


def _conv_kernel(val_ref, gate_ref, hval_ref, hgate_ref, cb_ref, cc_ref, ch_ref, hcc_ref, hch_ref,
                 wa_ref, ba_ref, ga_ref, bta_ref, wc_ref, a_ref, c_ref, abuf, cbuf, shifted):
    first = pl.program_id(1) == 0
    f32 = jnp.float32
    glu_h = hval_ref[0].astype(f32) * _sigmoid(hgate_ref[0].astype(f32))
    abuf[0:CONV_HALO, :] = jnp.where(first, 0.0, glu_h)
    abuf[CONV_HALO:, :] = val_ref[0].astype(f32) * _sigmoid(gate_ref[0].astype(f32))
    prod_h = hcc_ref[0].astype(f32) * hch_ref[0].astype(f32)
    cbuf[0:CONV_HALO, :] = jnp.where(first, 0.0, prod_h)
    cbuf[CONV_HALO:, :] = cc_ref[0].astype(f32) * ch_ref[0].astype(f32)

    span = CONV_HALO + CONV_TS - SUBLANES
    for j in range(1, SUBLANES):
        shifted[j - 1, 0:span, :] = abuf[j:j + span, :]

    a_off = CONV_HALO - (CONV_A_WIDTH - 1)
    c_off = CONV_HALO - (CONV_C_WIDTH - 1)
    groups = CONV_CHUNK // SUBLANES
    for r0 in range(0, CONV_TS, CONV_CHUNK):
        acc = jnp.zeros((groups, SUBLANES, BRANCH_W), f32)
        for k in range(CONV_A_WIDTH):
            j = (a_off + k) % SUBLANES
            base = r0 + a_off + k - j
            rows = abuf[base:base + CONV_CHUNK, :] if j == 0 else shifted[j - 1, base:base + CONV_CHUNK, :]
            tap = wa_ref[k * SUBLANES:(k + 1) * SUBLANES, :]
            acc = acc + tap[None] * rows.reshape(groups, SUBLANES, BRANCH_W)
        acc = acc.reshape(CONV_CHUNK, BRANCH_W)
        y = _layer_norm_rows(acc + ba_ref[...], ga_ref[...], bta_ref[...])
        a_ref[0, r0:r0 + CONV_CHUNK, :] = _silu(y).astype(a_ref.dtype)
        cacc = jnp.zeros((CONV_CHUNK, BRANCH_W), f32)
        for k in range(CONV_C_WIDTH):
            cacc = cacc + wc_ref[k:k + 1, :] * cbuf[r0 + c_off + k:r0 + c_off + k + CONV_CHUNK, :]
        c_ref[0, r0:r0 + CONV_CHUNK, :] = (cb_ref[0, r0:r0 + CONV_CHUNK, :].astype(f32) * cacc).astype(c_ref.dtype)


def _conv_branches(u, conv_a_w, conv_a_b, ln_a_g, ln_a_b, conv_c_w):
    b, s, _ = u.shape
    ts, halo = CONV_TS, CONV_HALO
    w = BRANCH_W
    ratio = ts // halo

    def main(col):
        return pl.BlockSpec((1, ts, w), lambda bi, i: (bi, i, col))

    def prev(col):
        return pl.BlockSpec((1, halo, w), lambda bi, i: (bi, jnp.maximum(i * ratio - 1, 0), col))

    def full(rows):
        return pl.BlockSpec((rows, w), lambda bi, i: (0, 0))

    cblk = OFF_C // w
    out = pl.pallas_call(
        _conv_kernel,
        grid=(b, s // ts),
        in_specs=[main(0), main(1), prev(0), prev(1),
                  main(cblk), main(cblk + 1), main(cblk + 2), prev(cblk + 1), prev(cblk + 2),
                  full(CONV_A_WIDTH * SUBLANES), full(1), full(1), full(1), full(CONV_C_WIDTH)],
        out_specs=[pl.BlockSpec((1, ts, w), lambda bi, i: (bi, i, 0)),
                   pl.BlockSpec((1, ts, w), lambda bi, i: (bi, i, 0))],
        out_shape=[jax.ShapeDtypeStruct((b, s, w), jnp.bfloat16),
                   jax.ShapeDtypeStruct((b, s, w), jnp.bfloat16)],
        scratch_shapes=[pltpu.VMEM((halo + ts, w), jnp.float32),
                        pltpu.VMEM((halo + ts, w), jnp.float32),
                        pltpu.VMEM((SUBLANES - 1, halo + ts, w), jnp.float32)],
        compiler_params=_cparams("parallel", "parallel"),
        name="conv_branches",
    )(u, u, u, u, u, u, u, u, u,
      jnp.repeat(conv_a_w, SUBLANES, axis=0), conv_a_b.reshape(1, w), ln_a_g.reshape(1, w), ln_a_b.reshape(1, w),
      conv_c_w)
    return out


def _attn_kernel(q0, q1, q2, k0, k1, k2, v0, v1, v2, o_ref, q32, k32, v32, nbuf, mbuf, zbuf, *, seq):
    f32, bf16 = jnp.float32, jnp.bfloat16
    blk = ATT_BLOCK
    scale = ATT_HEAD_DIM ** -0.5
    qi = lax.broadcasted_iota(jnp.int32, (blk, 2 * blk), 0)
    ki = lax.broadcasted_iota(jnp.int32, (blk, 2 * blk), 1)
    dist = blk + qi - ki
    valid2 = (dist >= 0) & (dist <= blk)
    valid1 = (lax.broadcasted_iota(jnp.int32, (blk, blk), 0)
              >= lax.broadcasted_iota(jnp.int32, (blk, blk), 1))

    for g, (r, qr, kr, vr) in enumerate(zip(ATT_DILATIONS, (q0, q1, q2), (k0, k1, k2), (v0, v1, v2))):
        length = seq // r
        nblk = length // blk
        if r > 1:
            q32[...] = qr[0].astype(f32)
            k32[...] = kr[0].astype(f32)
            v32[...] = vr[0].astype(f32)
        for c in range(r):
            for n in range(nblk):
                q_start = c + n * blk * r
                k_start = c + (n - 1) * blk * r if n > 0 else q_start
                k_len = 2 * blk if n > 0 else blk
                if r == 1:
                    q = qr[0, q_start:q_start + blk, :]
                    k = kr[0, k_start:k_start + k_len, :]
                    v = vr[0, k_start:k_start + k_len, :]
                else:
                    q = q32[pl.ds(q_start, blk, stride=r), :].astype(bf16)
                    k = k32[pl.ds(k_start, k_len, stride=r), :].astype(bf16)
                    v = v32[pl.ds(k_start, k_len, stride=r), :].astype(bf16)
                s = lax.dot_general(q, k, (((1,), (1,)), ((), ())), preferred_element_type=f32) * scale
                s = jnp.where(valid2 if n > 0 else valid1, s, NEG_INF)
                m = jnp.max(s, axis=-1, keepdims=True)
                p = jnp.exp(s - m)
                z = jnp.sum(p, axis=-1, keepdims=True)
                num = jnp.dot(p.astype(bf16), v, preferred_element_type=f32)
                if r == 1:
                    rows = pl.ds(q_start, blk)
                else:
                    rows = pl.ds(q_start, blk, stride=r)
                nbuf[g, rows, :] = num
                mbuf[g, rows, :] = jnp.broadcast_to(m, (blk, ATT_HEAD_DIM))
                zbuf[g, rows, :] = jnp.broadcast_to(z, (blk, ATT_HEAD_DIM))

    rows_per = 256
    for r0 in range(0, seq, rows_per):
        sl = slice(r0, r0 + rows_per)
        m_all = jnp.maximum(jnp.maximum(mbuf[0, sl, :], mbuf[1, sl, :]), mbuf[2, sl, :])
        num = jnp.zeros((rows_per, ATT_HEAD_DIM), f32)
        den = jnp.zeros((rows_per, ATT_HEAD_DIM), f32)
        for g in range(len(ATT_DILATIONS)):
            e = jnp.exp(mbuf[g, sl, :] - m_all)
            num = num + e * nbuf[g, sl, :]
            den = den + e * zbuf[g, sl, :]
        o_ref[0, sl, :] = (num / den).astype(o_ref.dtype)


def _attention(u):
    b, s, _ = u.shape
    dh = ATT_HEAD_DIM
    n_groups = len(ATT_DILATIONS)

    def spec(off, g):
        base = (off + g * BRANCH_W) // dh
        return pl.BlockSpec((1, s, dh), lambda bi, h: (bi, 0, base + h))

    in_specs = ([spec(OFF_Q, g) for g in range(n_groups)] + [spec(OFF_K, g) for g in range(n_groups)]
                + [spec(OFF_V, g) for g in range(n_groups)])
    return pl.pallas_call(
        functools.partial(_attn_kernel, seq=s),
        grid=(b, ATT_HEADS),
        in_specs=in_specs,
        out_specs=pl.BlockSpec((1, s, dh), lambda bi, h: (bi, 0, h)),
        out_shape=jax.ShapeDtypeStruct((b, s, BRANCH_W), jnp.bfloat16),
        scratch_shapes=[pltpu.VMEM((s, dh), jnp.float32)] * 3
        + [pltpu.VMEM((n_groups, s, dh), jnp.float32)] * 3,
        compiler_params=_cparams("parallel", "parallel"),
        name="dilated_attention",
    )(*([u] * 9))


def _merge_kernel(a_ref, att_ref, c_ref, g0_ref, g1_ref, g2_ref, wb_ref, o_ref):
    f32 = jnp.float32
    acc = None
    for n, (br, gr) in enumerate(((a_ref, g0_ref), (att_ref, g1_ref), (c_ref, g2_ref))):
        proj = jnp.dot(br[...], wb_ref[n], preferred_element_type=f32)
        term = _sigmoid(gr[...].astype(f32)) * proj
        acc = term if acc is None else acc + term
    o_ref[...] = acc.astype(o_ref.dtype)


def _branch_merge(a, att, c, u2d, w_branch, *, tm=512, tn=1024):
    t, w = a.shape
    d = D_MODEL
    gblk = OFF_G // tn

    def gate(n):
        return pl.BlockSpec((tm, tn), lambda j, i: (i, gblk + n * (d // tn) + j))

    row = pl.BlockSpec((tm, w), lambda j, i: (i, 0))
    return pl.pallas_call(
        _merge_kernel,
        grid=(d // tn, t // tm),
        in_specs=[row, row, row, gate(0), gate(1), gate(2),
                  pl.BlockSpec((N_BRANCHES, w, tn), lambda j, i: (0, 0, j))],
        out_specs=pl.BlockSpec((tm, tn), lambda j, i: (i, j)),
        out_shape=jax.ShapeDtypeStruct((t, d), jnp.bfloat16),
        compiler_params=_cparams("parallel", "parallel"),
        name="branch_merge",
    )(a, att, c, u2d, u2d, u2d, w_branch)


def _pack_rows(y, out_ref, rows):
    half = D_MODEL // 2
    lo = lax.bitcast_convert_type(y[:, :half].astype(jnp.bfloat16).astype(jnp.float32), jnp.int32)
    hi = lax.bitcast_convert_type(y[:, half:].astype(jnp.bfloat16).astype(jnp.float32), jnp.int32)
    word = (hi & HIGH_HALF_MASK) | lax.shift_right_logical(lo, 16)
    for c in range(PACK_SUBLANES):
        out_ref[pl.ds(c, rows, stride=PACK_SUBLANES), :] = word[:, c * LANES:(c + 1) * LANES]


def _unpack_rows(in_ref, rows):
    los, his = [], []
    for c in range(PACK_SUBLANES):
        word = in_ref[pl.ds(c, rows, stride=PACK_SUBLANES), :]
        los.append(lax.bitcast_convert_type(lax.shift_left(word, 16), jnp.float32).astype(jnp.bfloat16))
        his.append(lax.bitcast_convert_type(word & HIGH_HALF_MASK, jnp.float32).astype(jnp.bfloat16))
    return jnp.concatenate(los + his, axis=1)


def _outproj_ln_kernel(m_ref, w_ref, x_ref, g_ref, b_ref, o_ref, p_ref, *, alpha):
    y = jnp.dot(m_ref[...], w_ref[...], preferred_element_type=jnp.float32)
    x1 = _layer_norm_rows(alpha * x_ref[...] + y, g_ref[...], b_ref[...])
    o_ref[...] = x1
    _pack_rows(x1, p_ref, x1.shape[0])


def _outproj_ln(merged, w_out, x, g, b, *, alpha, tm=512):
    t, d = x.shape
    row = lambda i: (i, 0)
    const = lambda i: (0, 0)
    return pl.pallas_call(
        functools.partial(_outproj_ln_kernel, alpha=alpha),
        grid=(t // tm,),
        in_specs=[pl.BlockSpec((tm, d), row), pl.BlockSpec((d, d), const), pl.BlockSpec((tm, d), row),
                  pl.BlockSpec((1, d), const), pl.BlockSpec((1, d), const)],
        out_specs=[pl.BlockSpec((tm, d), row), pl.BlockSpec((tm * PACK_SUBLANES, LANES), row)],
        out_shape=[jax.ShapeDtypeStruct((t, d), jnp.float32),
                   jax.ShapeDtypeStruct((t * PACK_SUBLANES, LANES), jnp.int32)],
        compiler_params=_cparams("parallel"),
        name="out_proj_ln",
    )(merged, w_out, x, g.reshape(1, d), b.reshape(1, d))


def _first_argmax(vals, iota, axis, size):
    mx = jnp.max(vals, axis=axis, keepdims=True)
    idx = jnp.min(jnp.where(vals == mx, iota, size), axis=axis, keepdims=True)
    return mx, idx


def _router_kernel(x_ref, wt_ref, bias_ref, idx_ref, wts_ref, rank_ref, cnt_ref, carry):
    f32 = jnp.float32
    e, gsz, ng = MOE_EXPERTS, MOE_GROUP_SIZE, MOE_GROUPS
    tm = x_ref.shape[0]

    @pl.when(pl.program_id(0) == 0)
    def _():
        carry[...] = jnp.zeros_like(carry)

    logits = lax.dot_general(wt_ref[...], x_ref[...], (((1,), (1,)), ((), ())),
                             precision=lax.Precision.HIGHEST, preferred_element_type=f32)
    scores = _sigmoid(logits)
    choice = scores + bias_ref[...]
    grp = choice.reshape(ng, gsz, tm)
    iota_in = lax.broadcasted_iota(jnp.int32, (ng, gsz, tm), 1)
    m1, i1 = _first_argmax(grp, iota_in, 1, gsz)
    m2 = jnp.max(jnp.where(iota_in == i1, -jnp.inf, grp), axis=1, keepdims=True)
    grp_score = (m1 + m2).reshape(ng, tm)
    iota_g = lax.broadcasted_iota(jnp.int32, (ng, tm), 0)
    grp_mask = jnp.zeros((ng, tm), jnp.bool_)
    for _ in range(MOE_TOPK_GROUPS):
        _, gi = _first_argmax(grp_score, iota_g, 0, ng)
        hit = iota_g == gi
        grp_mask = grp_mask | hit
        grp_score = jnp.where(hit, -jnp.inf, grp_score)
    expert_mask = jnp.broadcast_to(grp_mask.reshape(ng, 1, tm), (ng, gsz, tm)).reshape(e, tm)
    masked = jnp.where(expert_mask, choice, NEG_INF)
    iota_e = lax.broadcasted_iota(jnp.int32, (e, tm), 0)
    idxs, ws, hits = [], [], []
    for _ in range(MOE_TOPK):
        _, ei = _first_argmax(masked, iota_e, 0, e)
        hit = iota_e == ei
        idxs.append(ei)
        hits.append(hit)
        ws.append(jnp.sum(jnp.where(hit, scores, 0.0), axis=0, keepdims=True))
        masked = jnp.where(hit, -jnp.inf, masked)
    w_all = jnp.concatenate(ws, axis=0)
    idx_ref[...] = jnp.concatenate(idxs, axis=0)
    wts_ref[...] = w_all / jnp.sum(w_all, axis=0, keepdims=True) * MOE_ROUTED_SCALE

    chosen = hits[0]
    for hit in hits[1:]:
        chosen = chosen | hit
    chosen_f = jnp.where(chosen, 1.0, 0.0)
    earlier = (lax.broadcasted_iota(jnp.int32, (tm, tm), 0) < lax.broadcasted_iota(jnp.int32, (tm, tm), 1))
    prefix = jnp.dot(chosen_f.astype(jnp.bfloat16), jnp.where(earlier, 1.0, 0.0).astype(jnp.bfloat16),
                     preferred_element_type=f32)
    base = carry[:, 0:1] + prefix
    ranks = [jnp.sum(jnp.where(hit, base, 0.0), axis=0, keepdims=True) for hit in hits]
    rank_ref[...] = jnp.concatenate(ranks, axis=0).astype(jnp.int32)
    carry[...] = carry[...] + jnp.sum(chosen_f, axis=1, keepdims=True)
    cnt_ref[...] = carry[...]


def _router(x1, router_w, router_bias, *, tm=512):
    t, d = x1.shape
    e = MOE_EXPERTS
    tok = pl.BlockSpec((MOE_TOPK, tm), lambda i: (0, i))
    return pl.pallas_call(
        _router_kernel,
        grid=(t // tm,),
        in_specs=[pl.BlockSpec((tm, d), lambda i: (i, 0)),
                  pl.BlockSpec((e, d), lambda i: (0, 0)),
                  pl.BlockSpec((e, 1), lambda i: (0, 0))],
        out_specs=[tok, tok, tok, pl.BlockSpec((e, LANES), lambda i: (0, 0))],
        out_shape=[jax.ShapeDtypeStruct((MOE_TOPK, t), jnp.int32),
                   jax.ShapeDtypeStruct((MOE_TOPK, t), jnp.float32),
                   jax.ShapeDtypeStruct((MOE_TOPK, t), jnp.int32),
                   jax.ShapeDtypeStruct((e, LANES), jnp.float32)],
        scratch_shapes=[pltpu.VMEM((e, LANES), jnp.float32)],
        compiler_params=_cparams("arbitrary"),
        name="router_topk",
    )(x1, router_w.T, router_bias.reshape(e, 1).astype(jnp.float32))


def _dispatch_kernel(pad_start_ref, pad_len_ref, nvalid_ref, dest_ref, xp_ref, xs_out, zrows, sem, pad_sem, *, tm):
    ps = PACK_SUBLANES

    @pl.when(pl.program_id(0) == 0)
    def _():
        zrows[...] = jnp.zeros_like(zrows)
        bits = [1 << s for s in reversed(range(MOE_ROWS.bit_length() - 1))]

        def fill(e, carry, *, wait):
            off = pad_start_ref[e]
            n = pad_len_ref[e]
            for bit in bits:
                has = (n & bit) != 0

                @pl.when(has)
                def _(off=off, bit=bit):
                    cp = pltpu.make_async_copy(zrows.at[pl.ds(0, bit * ps), :],
                                               xs_out.at[pl.ds(pl.multiple_of(off * ps, ps), bit * ps), :], pad_sem)
                    if wait:
                        cp.wait()
                    else:
                        cp.start()

                off = off + jnp.where(has, bit, 0)
            return carry

        half_rows = MOE_ROWS // 2

        def fill_tail(h, carry, *, wait):
            cp = pltpu.make_async_copy(zrows, xs_out.at[pl.ds(pl.multiple_of(h * half_rows * ps, ps), half_rows * ps), :],
                                       pad_sem)
            if wait:
                cp.wait()
            else:
                cp.start()
            return carry

        tail = (2 * nvalid_ref[0], 2 * (xs_out.shape[0] // (MOE_ROWS * ps)))
        lax.fori_loop(0, MOE_EXPERTS, functools.partial(fill, wait=False), 0)
        lax.fori_loop(*tail, functools.partial(fill_tail, wait=False), 0)
        lax.fori_loop(0, MOE_EXPERTS, functools.partial(fill, wait=True), 0)
        lax.fori_loop(*tail, functools.partial(fill_tail, wait=True), 0)

    for k in range(MOE_TOPK):
        def body(pair, carry, k=k):
            for prio in range(DMA_PRIORITIES):
                tok = pair * DMA_PRIORITIES + prio
                row = dest_ref[0, 0, k * tm + tok]
                pltpu.make_async_copy(xp_ref.at[pl.ds(pl.multiple_of(tok * ps, ps), ps), :],
                                      xs_out.at[pl.ds(pl.multiple_of(row * ps, ps), ps), :],
                                      sem).start(priority=prio)
            return carry

        lax.fori_loop(0, tm // DMA_PRIORITIES, body, 0, unroll=4)
    for _ in range(MOE_TOPK):
        pltpu.make_async_copy(xp_ref, xs_out.at[pl.ds(0, tm * ps), :], sem).wait()


def _dispatch(x1p, dest_kt, pad_start, pad_len, nvalid, n_rows, *, tm=512):
    k, t = dest_kt.shape
    ps = PACK_SUBLANES
    nsteps = t // tm
    dest3 = dest_kt.reshape(k, nsteps, tm).transpose(1, 0, 2).reshape(nsteps, 1, k * tm)
    grid_spec = pltpu.PrefetchScalarGridSpec(
        num_scalar_prefetch=3,
        grid=(nsteps,),
        in_specs=[pl.BlockSpec((1, 1, k * tm), lambda i, *_: (i, 0, 0), memory_space=pltpu.SMEM),
                  pl.BlockSpec((tm * ps, LANES), lambda i, *_: (i, 0))],
        out_specs=pl.BlockSpec(memory_space=pl.ANY),
        scratch_shapes=[pltpu.VMEM((MOE_ROWS // 2 * ps, LANES), jnp.int32),
                        pltpu.SemaphoreType.DMA(()), pltpu.SemaphoreType.DMA(())],
    )
    return pl.pallas_call(
        functools.partial(_dispatch_kernel, tm=tm),
        grid_spec=grid_spec,
        out_shape=jax.ShapeDtypeStruct((n_rows * ps, LANES), jnp.int32),
        compiler_params=_cparams("arbitrary"),
        name="dispatch_rows",
    )(pad_start, pad_len, nvalid, dest3, x1p)


def _expert_kernel(blk_e_ref, nvalid_ref, xs_ref, wgu_ref, wd_ref, y_ref):
    del blk_e_ref
    i = pl.program_id(0)
    rows = MOE_ROWS

    @pl.when(i < nvalid_ref[0])
    def _():
        xb = _unpack_rows(xs_ref, rows)
        h = jnp.dot(xb, wgu_ref[...], preferred_element_type=jnp.float32)
        act = _silu(h[:, :MOE_FFN]) * h[:, MOE_FFN:]
        y = jnp.dot(act.astype(jnp.bfloat16), wd_ref[...], preferred_element_type=jnp.float32)
        _pack_rows(y, y_ref, rows)

    @pl.when(i >= nvalid_ref[0])
    def _():
        y_ref[...] = jnp.zeros_like(y_ref)


def _routed_experts(xs, blk_expert, nvalid, wgu, wd):
    nb = blk_expert.shape[0]
    rows, d = MOE_ROWS, D_MODEL
    grid_spec = pltpu.PrefetchScalarGridSpec(
        num_scalar_prefetch=2,
        grid=(nb,),
        in_specs=[pl.BlockSpec((rows * PACK_SUBLANES, LANES), lambda i, be, nv: (jnp.minimum(i, nv[0] - 1), 0)),
                  pl.BlockSpec((None, d, 2 * MOE_FFN), lambda i, be, nv: (be[i], 0, 0)),
                  pl.BlockSpec((None, MOE_FFN, d), lambda i, be, nv: (be[i], 0, 0))],
        out_specs=pl.BlockSpec((rows * PACK_SUBLANES, LANES), lambda i, be, nv: (i, 0)),
    )
    return pl.pallas_call(
        _expert_kernel,
        grid_spec=grid_spec,
        out_shape=jax.ShapeDtypeStruct((nb * rows * PACK_SUBLANES, LANES), jnp.int32),
        compiler_params=_cparams("parallel"),
        name="routed_experts",
    )(blk_expert, nvalid, xs, wgu, wd)


def _shared_kernel(x_ref, wgu_ref, wd_ref, o_ref):
    xb = x_ref[...].astype(jnp.bfloat16)
    h = jnp.dot(xb, wgu_ref[...], preferred_element_type=jnp.float32)
    act = _silu(h[:, :MOE_FFN]) * h[:, MOE_FFN:]
    o_ref[...] = jnp.dot(act.astype(jnp.bfloat16), wd_ref[...], preferred_element_type=jnp.float32)


def _shared_expert(x1, wgu, wd, *, tm=512):
    t, d = x1.shape
    return pl.pallas_call(
        _shared_kernel,
        grid=(t // tm,),
        in_specs=[pl.BlockSpec((tm, d), lambda i: (i, 0)),
                  pl.BlockSpec((d, 2 * MOE_FFN), lambda i: (0, 0)),
                  pl.BlockSpec((MOE_FFN, d), lambda i: (0, 0))],
        out_specs=pl.BlockSpec((tm, d), lambda i: (i, 0)),
        out_shape=jax.ShapeDtypeStruct((t, d), jnp.float32),
        compiler_params=_cparams("parallel"),
        name="shared_expert",
    )(x1, wgu, wd)


def _combine_kernel(dest_ref, dest_next_ref, ys_hbm, w_ref, x_ref, sh_ref, g_ref, b_ref, o_ref, ob_ref,
                    ybuf0, ybuf1, acc, sem, *, alpha, nsteps):
    i = pl.program_id(0)
    tc, ps = COMBINE_TOK, PACK_SUBLANES
    n_rows = tc * MOE_TOPK
    half = D_MODEL // 2
    f32 = jnp.float32
    bufs = (ybuf0, ybuf1)

    def row_copy(d_ref, r, slot):
        row = d_ref[0, 0, r]
        return pltpu.make_async_copy(ys_hbm.at[pl.ds(pl.multiple_of(row * ps, ps), ps), :],
                                     bufs[slot].at[pl.ds(pl.multiple_of(r * ps, ps), ps), :], sem.at[slot])

    def wait_block(slot):
        pltpu.make_async_copy(ys_hbm.at[pl.ds(0, n_rows * ps), :], bufs[slot], sem.at[slot]).wait()

    @pl.when(i == 0)
    def _():
        def body(pair, carry):
            for prio in range(DMA_PRIORITIES):
                row_copy(dest_ref, pair * DMA_PRIORITIES + prio, 0).start(priority=prio)
            return carry
        lax.fori_loop(0, n_rows // DMA_PRIORITIES, body, 0, unroll=4)

    rows_per_chunk = n_rows // ps
    for slot in range(2):
        @pl.when(lax.rem(i, 2) == slot)
        def _(slot=slot):
            wait_block(slot)
            for c in range(ps):
                for r in range(c * rows_per_chunk, (c + 1) * rows_per_chunk):
                    row_copy(dest_next_ref, r, 1 - slot).start(priority=r % DMA_PRIORITIES)
                lo_cols = slice(c * LANES, (c + 1) * LANES)
                hi_cols = slice(half + c * LANES, half + (c + 1) * LANES)
                a_lo = alpha * x_ref[:, lo_cols] + sh_ref[:, lo_cols]
                a_hi = alpha * x_ref[:, hi_cols] + sh_ref[:, hi_cols]
                for k in range(MOE_TOPK):
                    word = bufs[slot][pl.ds(k * tc * ps + c, tc, stride=ps), :]
                    wk = w_ref[:, k:k + 1]
                    a_lo = a_lo + wk * lax.bitcast_convert_type(lax.shift_left(word, 16), f32)
                    a_hi = a_hi + wk * lax.bitcast_convert_type(word & HIGH_HALF_MASK, f32)
                acc[:, lo_cols] = a_lo
                acc[:, hi_cols] = a_hi

            @pl.when(i == nsteps - 1)
            def _():
                wait_block(1 - slot)

    x2 = _layer_norm_rows(acc[...], g_ref[...], b_ref[...])
    o_ref[...] = x2
    ob_ref[...] = x2.astype(ob_ref.dtype)


def _combine_ln(ys, dest_kt, wts_tk, x1, shared, g, b, *, alpha):
    t, d = x1.shape
    tc = COMBINE_TOK
    nsteps = t // tc
    dest3 = dest_kt.reshape(MOE_TOPK, nsteps, tc).transpose(1, 0, 2).reshape(nsteps, 1, MOE_TOPK * tc)
    smem_blk = lambda f: pl.BlockSpec((1, 1, MOE_TOPK * tc), f, memory_space=pltpu.SMEM)
    row = lambda i: (i, 0)
    const = lambda i: (0, 0)
    return pl.pallas_call(
        functools.partial(_combine_kernel, alpha=alpha, nsteps=nsteps),
        grid=(nsteps,),
        in_specs=[smem_blk(lambda i: (i, 0, 0)),
                  smem_blk(lambda i: (jnp.minimum(i + 1, nsteps - 1), 0, 0)),
                  pl.BlockSpec(memory_space=pl.ANY),
                  pl.BlockSpec((tc, MOE_TOPK), row),
                  pl.BlockSpec((tc, d), row), pl.BlockSpec((tc, d), row),
                  pl.BlockSpec((1, d), const), pl.BlockSpec((1, d), const)],
        out_specs=[pl.BlockSpec((tc, d), row), pl.BlockSpec((tc, d), row)],
        out_shape=[jax.ShapeDtypeStruct((t, d), jnp.float32), jax.ShapeDtypeStruct((t, d), jnp.bfloat16)],
        scratch_shapes=[pltpu.VMEM((MOE_TOPK * tc * PACK_SUBLANES, LANES), jnp.int32),
                        pltpu.VMEM((MOE_TOPK * tc * PACK_SUBLANES, LANES), jnp.int32),
                        pltpu.VMEM((tc, d), jnp.float32),
                        pltpu.SemaphoreType.DMA((2,))],
        compiler_params=_cparams("arbitrary"),
        name="combine_ln",
    )(dest3, dest3, ys, wts_tk, x1, shared, g.reshape(1, d), b.reshape(1, d))


def _dispatch_plan(idx_kt, rank_kt, counts):
    k, t = idx_kt.shape
    e, rows = MOE_EXPERTS, MOE_ROWS
    nb = k * t // rows + e
    padded = (counts + rows - 1) // rows * rows
    pend = jnp.cumsum(padded)
    pstart = pend - padded
    blk_start = jnp.arange(nb, dtype=jnp.int32) * rows
    blk_expert = jnp.minimum(jnp.sum(pend[None, :] <= blk_start[:, None], axis=1), e - 1).astype(jnp.int32)
    nvalid = (pend[-1] // rows).astype(jnp.int32).reshape(1)
    onehot = idx_kt[:, :, None] == jnp.arange(e, dtype=jnp.int32)
    dest = rank_kt + jnp.sum(jnp.where(onehot, pstart, 0), axis=-1).astype(jnp.int32)
    pad_start = (pstart + counts).astype(jnp.int32)
    pad_len = (padded - counts).astype(jnp.int32)
    return blk_expert, nvalid, dest, pad_start, pad_len, nb


def kernel(x, w_in, conv_a_w, conv_a_b, ln_a_g, ln_a_b, conv_c_w, w_branch, w_out, ln1_g, ln1_b, router_w,
           router_bias, w_gate, w_up, w_down, ws_gate, ws_up, ws_down, ln2_g, ln2_b):
    b, s, d = x.shape
    depth = w_in.shape[0]
    t = b * s
    alpha = (DEPTH_ALPHA_BASE * depth) ** 0.25
    bf16 = jnp.bfloat16
    xf = x.reshape(t, d)
    xb = xf.astype(bf16)
    for l in range(depth):
        u = _matmul(xb, w_in[l].astype(bf16), tm=min(t, 1024), tn=1024, out_dtype=bf16)
        u3 = u.reshape(b, s, N_IN)
        a, c = _conv_branches(u3, conv_a_w[l], conv_a_b[l], ln_a_g[l], ln_a_b[l], conv_c_w[l])
        att = _attention(u3)
        merged = _branch_merge(a.reshape(t, BRANCH_W), att.reshape(t, BRANCH_W), c.reshape(t, BRANCH_W), u,
                               w_branch[l].astype(bf16))
        x1, x1p = _outproj_ln(merged, w_out[l].astype(bf16), xf, ln1_g[l], ln1_b[l], alpha=alpha)
        idx_kt, wts_kt, rank_kt, counts = _router(x1, router_w[l], router_bias[l])
        blk_expert, nvalid, dest, pad_start, pad_len, nb = _dispatch_plan(idx_kt, rank_kt,
                                                                          counts[:, 0].astype(jnp.int32))
        xs = _dispatch(x1p, dest, pad_start, pad_len, nvalid, nb * MOE_ROWS)
        wgu = jnp.concatenate([w_gate[l], w_up[l]], axis=-1).astype(bf16)
        ys = _routed_experts(xs, blk_expert, nvalid, wgu, w_down[l].astype(bf16))
        wsgu = jnp.concatenate([ws_gate[l], ws_up[l]], axis=-1).astype(bf16)
        shared = _shared_expert(x1, wsgu, ws_down[l].astype(bf16))
        xf, xb = _combine_ln(ys, dest, wts_kt.T, x1, shared, ln2_g[l], ln2_b[l], alpha=alpha)
    return xf.reshape(b, s, d)
```

```python
import functools

import jax
import jax.numpy as jnp
from jax import lax
from jax.experimental import pallas as pl
from jax.experimental.pallas import tpu as pltpu

D_MODEL = 2048
BRANCH_W = D_MODEL // 2
N_BRANCHES = 3
CONV_A_WIDTH = 31
CONV_C_WIDTH = 3
ATT_DILATIONS = (1, 4, 16)
ATT_BLOCK = 128
ATT_HEADS = 8
ATT_HEAD_DIM = BRANCH_W // ATT_HEADS
ATT_W = len(ATT_DILATIONS) * BRANCH_W
OFF_Q = 2 * BRANCH_W
OFF_K = OFF_Q + ATT_W
OFF_V = OFF_K + ATT_W
OFF_C = OFF_V + ATT_W
OFF_G = OFF_C + 3 * BRANCH_W
N_IN = OFF_G + N_BRANCHES * D_MODEL
MOE_EXPERTS = 64
MOE_TOPK = 8
MOE_GROUPS = 8
MOE_GROUP_SIZE = MOE_EXPERTS // MOE_GROUPS
MOE_TOPK_GROUPS = 4
MOE_FFN = D_MODEL // 8
MOE_ROUTED_SCALE = 2.5
LN_EPS = 1e-5
NEG_INF = -1e30
DEPTH_ALPHA_BASE = 2.0

LANES = 128
VMEM_LIMIT_BYTES = 52 * 1024 * 1024
MOE_ROWS = 512
PACK_SUBLANES = D_MODEL // 2 // LANES
SUBLANES = 8
HIGH_HALF_MASK = -65536
CONV_TS = 256
CONV_HALO = 32
CONV_CHUNK = 32
COMBINE_TOK = 128
DMA_PRIORITIES = 2


def _cparams(*sem):
    return pltpu.CompilerParams(dimension_semantics=sem, vmem_limit_bytes=VMEM_LIMIT_BYTES)


def _sigmoid(v):
    return 1.0 / (1.0 + jnp.exp(-v))


def _silu(v):
    return v * _sigmoid(v)


def _layer_norm_rows(v, g, b):
    mu = jnp.mean(v, axis=-1, keepdims=True)
    vc = v - mu
    var = jnp.mean(vc * vc, axis=-1, keepdims=True)
    return vc * lax.rsqrt(var + LN_EPS) * g + b


def _matmul_kernel(x_ref, w_ref, o_ref):
    o_ref[...] = jnp.dot(x_ref[...], w_ref[...], preferred_element_type=jnp.float32).astype(o_ref.dtype)


def _matmul(x, w, *, tm, tn, out_dtype):
    m, k = x.shape
    _, n = w.shape
    return pl.pallas_call(
        _matmul_kernel,
        grid=(m // tm, n // tn),
        in_specs=[pl.BlockSpec((tm, k), lambda i, j: (i, 0)),
                  pl.BlockSpec((k, tn), lambda i, j: (0, j))],
        out_specs=pl.BlockSpec((tm, tn), lambda i, j: (i, j)),
        out_shape=jax.ShapeDtypeStruct((m, n), out_dtype),
        compiler_params=_cparams("parallel", "parallel"),
        name="in_proj",
    )(x, w)


def _conv_kernel(val_ref, gate_ref, hval_ref, hgate_ref, cb_ref, cc_ref, ch_ref, hcc_ref, hch_ref,
                 wa_ref, ba_ref, ga_ref, bta_ref, wc_ref, a_ref, c_ref, abuf, cbuf, shifted):
    first = pl.program_id(1) == 0
    f32 = jnp.float32
    glu_h = hval_ref[0].astype(f32) * _sigmoid(hgate_ref[0].astype(f32))
    abuf[0:CONV_HALO, :] = jnp.where(first, 0.0, glu_h)
    abuf[CONV_HALO:, :] = val_ref[0].astype(f32) * _sigmoid(gate_ref[0].astype(f32))
    prod_h = hcc_ref[0].astype(f32) * hch_ref[0].astype(f32)
    cbuf[0:CONV_HALO, :] = jnp.where(first, 0.0, prod_h)
    cbuf[CONV_HALO:, :] = cc_ref[0].astype(f32) * ch_ref[0].astype(f32)

    span = CONV_HALO + CONV_TS - SUBLANES
    for j in range(1, SUBLANES):
        shifted[j - 1, 0:span, :] = abuf[j:j + span, :]

    a_off = CONV_HALO - (CONV_A_WIDTH - 1)
    c_off = CONV_HALO - (CONV_C_WIDTH - 1)
    groups = CONV_CHUNK // SUBLANES
    for r0 in range(0, CONV_TS, CONV_CHUNK):
        acc = jnp.zeros((groups, SUBLANES, BRANCH_W), f32)
        for k in range(CONV_A_WIDTH):
            j = (a_off + k) % SUBLANES
            base = r0 + a_off + k - j
            rows = abuf[base:base + CONV_CHUNK, :] if j == 0 else shifted[j - 1, base:base + CONV_CHUNK, :]
            tap = wa_ref[k * SUBLANES:(k + 1) * SUBLANES, :]
            acc = acc + tap[None] * rows.reshape(groups, SUBLANES, BRANCH_W)
        acc = acc.reshape(CONV_CHUNK, BRANCH_W)
        y = _layer_norm_rows(acc + ba_ref[...], ga_ref[...], bta_ref[...])
        a_ref[0, r0:r0 + CONV_CHUNK, :] = _silu(y).astype(a_ref.dtype)
        cacc = jnp.zeros((CONV_CHUNK, BRANCH_W), f32)
        for k in range(CONV_C_WIDTH):
            cacc = cacc + wc_ref[k:k + 1, :] * cbuf[r0 + c_off + k:r0 + c_off + k + CONV_CHUNK, :]
        c_ref[0, r0:r0 + CONV_CHUNK, :] = (cb_ref[0, r0:r0 + CONV_CHUNK, :].astype(f32) * cacc).astype(c_ref.dtype)


def _conv_branches(u, conv_a_w, conv_a_b, ln_a_g, ln_a_b, conv_c_w):
    b, s, _ = u.shape
    ts, halo = CONV_TS, CONV_HALO
    w = BRANCH_W
    ratio = ts // halo

    def main(col):
        return pl.BlockSpec((1, ts, w), lambda bi, i: (bi, i, col))

    def prev(col):
        return pl.BlockSpec((1, halo, w), lambda bi, i: (bi, jnp.maximum(i * ratio - 1, 0), col))

    def full(rows):
        return pl.BlockSpec((rows, w), lambda bi, i: (0, 0))

    cblk = OFF_C // w
    out = pl.pallas_call(
        _conv_kernel,
        grid=(b, s // ts),
        in_specs=[main(0), main(1), prev(0), prev(1),
                  main(cblk), main(cblk + 1), main(cblk + 2), prev(cblk + 1), prev(cblk + 2),
                  full(CONV_A_WIDTH * SUBLANES), full(1), full(1), full(1), full(CONV_C_WIDTH)],
        out_specs=[pl.BlockSpec((1, ts, w), lambda bi, i: (bi, i, 0)),
                   pl.BlockSpec((1, ts, w), lambda bi, i: (bi, i, 0))],
        out_shape=[jax.ShapeDtypeStruct((b, s, w), jnp.bfloat16),
                   jax.ShapeDtypeStruct((b, s, w), jnp.bfloat16)],
        scratch_shapes=[pltpu.VMEM((halo + ts, w), jnp.float32),
                        pltpu.VMEM((halo + ts, w), jnp.float32),
                        pltpu.VMEM((SUBLANES - 1, halo + ts, w), jnp.float32)],
        compiler_params=_cparams("parallel", "parallel"),
        name="conv_branches",
    )(u, u, u, u, u, u, u, u, u,
      jnp.repeat(conv_a_w, SUBLANES, axis=0), conv_a_b.reshape(1, w), ln_a_g.reshape(1, w), ln_a_b.reshape(1, w),
      conv_c_w)
    return out


def _attn_kernel(q0, q1, q2, k0, k1, k2, v0, v1, v2, o_ref, q32, k32, v32, nbuf, mbuf, zbuf, *, seq):
    f32, bf16 = jnp.float32, jnp.bfloat16
    blk = ATT_BLOCK
    scale = ATT_HEAD_DIM ** -0.5
    qi = lax.broadcasted_iota(jnp.int32, (blk, 2 * blk), 0)
    ki = lax.broadcasted_iota(jnp.int32, (blk, 2 * blk), 1)
    dist = blk + qi - ki
    valid2 = (dist >= 0) & (dist <= blk)
    valid1 = (lax.broadcasted_iota(jnp.int32, (blk, blk), 0)
              >= lax.broadcasted_iota(jnp.int32, (blk, blk), 1))

    for g, (r, qr, kr, vr) in enumerate(zip(ATT_DILATIONS, (q0, q1, q2), (k0, k1, k2), (v0, v1, v2))):
        length = seq // r
        nblk = length // blk
        if r > 1:
            q32[...] = qr[0].astype(f32)
            k32[...] = kr[0].astype(f32)
            v32[...] = vr[0].astype(f32)
        for c in range(r):
            for n in range(nblk):
                q_start = c + n * blk * r
                k_start = c + (n - 1) * blk * r if n > 0 else q_start
                k_len = 2 * blk if n > 0 else blk
                if r == 1:
                    q = qr[0, q_start:q_start + blk, :]
                    k = kr[0, k_start:k_start + k_len, :]
                    v = vr[0, k_start:k_start + k_len, :]
                else:
                    q = q32[pl.ds(q_start, blk, stride=r), :].astype(bf16)
                    k = k32[pl.ds(k_start, k_len, stride=r), :].astype(bf16)
                    v = v32[pl.ds(k_start, k_len, stride=r), :].astype(bf16)
                s = lax.dot_general(q, k, (((1,), (1,)), ((), ())), preferred_element_type=f32) * scale
                s = jnp.where(valid2 if n > 0 else valid1, s, NEG_INF)
                m = jnp.max(s, axis=-1, keepdims=True)
                p = jnp.exp(s - m)
                z = jnp.sum(p, axis=-1, keepdims=True)
                num = jnp.dot(p.astype(bf16), v, preferred_element_type=f32)
                if r == 1:
                    rows = pl.ds(q_start, blk)
                else:
                    rows = pl.ds(q_start, blk, stride=r)
                nbuf[g, rows, :] = num
                mbuf[g, rows, :] = jnp.broadcast_to(m, (blk, ATT_HEAD_DIM))
                zbuf[g, rows, :] = jnp.broadcast_to(z, (blk, ATT_HEAD_DIM))

    rows_per = 256
    for r0 in range(0, seq, rows_per):
        sl = slice(r0, r0 + rows_per)
        m_all = jnp.maximum(jnp.maximum(mbuf[0, sl, :], mbuf[1, sl, :]), mbuf[2, sl, :])
        num = jnp.zeros((rows_per, ATT_HEAD_DIM), f32)
        den = jnp.zeros((rows_per, ATT_HEAD_DIM), f32)
        for g in range(len(ATT_DILATIONS)):
            e = jnp.exp(mbuf[g, sl, :] - m_all)
            num = num + e * nbuf[g, sl, :]
            den = den + e * zbuf[g, sl, :]
        o_ref[0, sl, :] = (num / den).astype(o_ref.dtype)


def _attention(u):
    b, s, _ = u.shape
    dh = ATT_HEAD_DIM
    n_groups = len(ATT_DILATIONS)

    def spec(off, g):
        base = (off + g * BRANCH_W) // dh
        return pl.BlockSpec((1, s, dh), lambda bi, h: (bi, 0, base + h))

    in_specs = ([spec(OFF_Q, g) for g in range(n_groups)] + [spec(OFF_K, g) for g in range(n_groups)]
                + [spec(OFF_V, g) for g in range(n_groups)])
    return pl.pallas_call(
        functools.partial(_attn_kernel, seq=s),
        grid=(b, ATT_HEADS),
        in_specs=in_specs,
        out_specs=pl.BlockSpec((1, s, dh), lambda bi, h: (bi, 0, h)),
        out_shape=jax.ShapeDtypeStruct((b, s, BRANCH_W), jnp.bfloat16),
        scratch_shapes=[pltpu.VMEM((s, dh), jnp.float32)] * 3
        + [pltpu.VMEM((n_groups, s, dh), jnp.float32)] * 3,
        compiler_params=_cparams("parallel", "parallel"),
        name="dilated_attention",
    )(*([u] * 9))


def _merge_kernel(a_ref, att_ref, c_ref, g0_ref, g1_ref, g2_ref, wb_ref, o_ref):
    f32 = jnp.float32
    acc = None
    for n, (br, gr) in enumerate(((a_ref, g0_ref), (att_ref, g1_ref), (c_ref, g2_ref))):
        proj = jnp.dot(br[...], wb_ref[n], preferred_element_type=f32)
        term = _sigmoid(gr[...].astype(f32)) * proj
        acc = term if acc is None else acc + term
    o_ref[...] = acc.astype(o_ref.dtype)


def _branch_merge(a, att, c, u2d, w_branch, *, tm=512, tn=1024):
    t, w = a.shape
    d = D_MODEL
    gblk = OFF_G // tn

    def gate(n):
        return pl.BlockSpec((tm, tn), lambda j, i: (i, gblk + n * (d // tn) + j))

    row = pl.BlockSpec((tm, w), lambda j, i: (i, 0))
    return pl.pallas_call(
        _merge_kernel,
        grid=(d // tn, t // tm),
        in_specs=[row, row, row, gate(0), gate(1), gate(2),
                  pl.BlockSpec((N_BRANCHES, w, tn), lambda j, i: (0, 0, j))],
        out_specs=pl.BlockSpec((tm, tn), lambda j, i: (i, j)),
        out_shape=jax.ShapeDtypeStruct((t, d), jnp.bfloat16),
        compiler_params=_cparams("parallel", "parallel"),
        name="branch_merge",
    )(a, att, c, u2d, u2d, u2d, w_branch)


def _pack_rows(y, out_ref, rows):
    half = D_MODEL // 2
    lo = lax.bitcast_convert_type(y[:, :half].astype(jnp.bfloat16).astype(jnp.float32), jnp.int32)
    hi = lax.bitcast_convert_type(y[:, half:].astype(jnp.bfloat16).astype(jnp.float32), jnp.int32)
    word = (hi & HIGH_HALF_MASK) | lax.shift_right_logical(lo, 16)
    for c in range(PACK_SUBLANES):
        out_ref[pl.ds(c, rows, stride=PACK_SUBLANES), :] = word[:, c * LANES:(c + 1) * LANES]


def _unpack_rows(in_ref, rows):
    los, his = [], []
    for c in range(PACK_SUBLANES):
        word = in_ref[pl.ds(c, rows, stride=PACK_SUBLANES), :]
        los.append(lax.bitcast_convert_type(lax.shift_left(word, 16), jnp.float32).astype(jnp.bfloat16))
        his.append(lax.bitcast_convert_type(word & HIGH_HALF_MASK, jnp.float32).astype(jnp.bfloat16))
    return jnp.concatenate(los + his, axis=1)


def _outproj_ln_kernel(m_ref, w_ref, x_ref, g_ref, b_ref, o_ref, p_ref, *, alpha):
    y = jnp.dot(m_ref[...], w_ref[...], preferred_element_type=jnp.float32)
    x1 = _layer_norm_rows(alpha * x_ref[...] + y, g_ref[...], b_ref[...])
    o_ref[...] = x1
    _pack_rows(x1, p_ref, x1.shape[0])


def _outproj_ln(merged, w_out, x, g, b, *, alpha, tm=512):
    t, d = x.shape
    row = lambda i: (i, 0)
    const = lambda i: (0, 0)
    return pl.pallas_call(
        functools.partial(_outproj_ln_kernel, alpha=alpha),
        grid=(t // tm,),
        in_specs=[pl.BlockSpec((tm, d), row), pl.BlockSpec((d, d), const), pl.BlockSpec((tm, d), row),
                  pl.BlockSpec((1, d), const), pl.BlockSpec((1, d), const)],
        out_specs=[pl.BlockSpec((tm, d), row), pl.BlockSpec((tm * PACK_SUBLANES, LANES), row)],
        out_shape=[jax.ShapeDtypeStruct((t, d), jnp.float32),
                   jax.ShapeDtypeStruct((t * PACK_SUBLANES, LANES), jnp.int32)],
        compiler_params=_cparams("parallel"),
        name="out_proj_ln",
    )(merged, w_out, x, g.reshape(1, d), b.reshape(1, d))


def _first_argmax(vals, iota, axis, size):
    mx = jnp.max(vals, axis=axis, keepdims=True)
    idx = jnp.min(jnp.where(vals == mx, iota, size), axis=axis, keepdims=True)
    return mx, idx


def _router_kernel(x_ref, wt_ref, bias_ref, idx_ref, wts_ref, rank_ref, cnt_ref, carry):
    f32 = jnp.float32
    e, gsz, ng = MOE_EXPERTS, MOE_GROUP_SIZE, MOE_GROUPS
    tm = x_ref.shape[0]

    @pl.when(pl.program_id(0) == 0)
    def _():
        carry[...] = jnp.zeros_like(carry)

    logits = lax.dot_general(wt_ref[...], x_ref[...], (((1,), (1,)), ((), ())),
                             precision=lax.Precision.HIGHEST, preferred_element_type=f32)
    scores = _sigmoid(logits)
    choice = scores + bias_ref[...]
    grp = choice.reshape(ng, gsz, tm)
    iota_in = lax.broadcasted_iota(jnp.int32, (ng, gsz, tm), 1)
    m1, i1 = _first_argmax(grp, iota_in, 1, gsz)
    m2 = jnp.max(jnp.where(iota_in == i1, -jnp.inf, grp), axis=1, keepdims=True)
    grp_score = (m1 + m2).reshape(ng, tm)
    iota_g = lax.broadcasted_iota(jnp.int32, (ng, tm), 0)
    grp_mask = jnp.zeros((ng, tm), jnp.bool_)
    for _ in range(MOE_TOPK_GROUPS):
        _, gi = _first_argmax(grp_score, iota_g, 0, ng)
        hit = iota_g == gi
        grp_mask = grp_mask | hit
        grp_score = jnp.where(hit, -jnp.inf, grp_score)
    expert_mask = jnp.broadcast_to(grp_mask.reshape(ng, 1, tm), (ng, gsz, tm)).reshape(e, tm)
    masked = jnp.where(expert_mask, choice, NEG_INF)
    iota_e = lax.broadcasted_iota(jnp.int32, (e, tm), 0)
    idxs, ws, hits = [], [], []
    for _ in range(MOE_TOPK):
        _, ei = _first_argmax(masked, iota_e, 0, e)
        hit = iota_e == ei
        idxs.append(ei)
        hits.append(hit)
        ws.append(jnp.sum(jnp.where(hit, scores, 0.0), axis=0, keepdims=True))
        masked = jnp.where(hit, -jnp.inf, masked)
    w_all = jnp.concatenate(ws, axis=0)
    idx_ref[...] = jnp.concatenate(idxs, axis=0)
    wts_ref[...] = w_all / jnp.sum(w_all, axis=0, keepdims=True) * MOE_ROUTED_SCALE

    chosen = hits[0]
    for hit in hits[1:]:
        chosen = chosen | hit
    chosen_f = jnp.where(chosen, 1.0, 0.0)
    earlier = (lax.broadcasted_iota(jnp.int32, (tm, tm), 0) < lax.broadcasted_iota(jnp.int32, (tm, tm), 1))
    prefix = jnp.dot(chosen_f.astype(jnp.bfloat16), jnp.where(earlier, 1.0, 0.0).astype(jnp.bfloat16),
                     preferred_element_type=f32)
    base = carry[:, 0:1] + prefix
    ranks = [jnp.sum(jnp.where(hit, base, 0.0), axis=0, keepdims=True) for hit in hits]
    rank_ref[...] = jnp.concatenate(ranks, axis=0).astype(jnp.int32)
    carry[...] = carry[...] + jnp.sum(chosen_f, axis=1, keepdims=True)
    cnt_ref[...] = carry[...]


def _router(x1, router_w, router_bias, *, tm=512):
    t, d = x1.shape
    e = MOE_EXPERTS
    tok = pl.BlockSpec((MOE_TOPK, tm), lambda i: (0, i))
    return pl.pallas_call(
        _router_kernel,
        grid=(t // tm,),
        in_specs=[pl.BlockSpec((tm, d), lambda i: (i, 0)),
                  pl.BlockSpec((e, d), lambda i: (0, 0)),
                  pl.BlockSpec((e, 1), lambda i: (0, 0))],
        out_specs=[tok, tok, tok, pl.BlockSpec((e, LANES), lambda i: (0, 0))],
        out_shape=[jax.ShapeDtypeStruct((MOE_TOPK, t), jnp.int32),
                   jax.ShapeDtypeStruct((MOE_TOPK, t), jnp.float32),
                   jax.ShapeDtypeStruct((MOE_TOPK, t), jnp.int32),
                   jax.ShapeDtypeStruct((e, LANES), jnp.float32)],
        scratch_shapes=[pltpu.VMEM((e, LANES), jnp.float32)],
        compiler_params=_cparams("arbitrary"),
        name="router_topk",
    )(x1, router_w.T, router_bias.reshape(e, 1).astype(jnp.float32))


def _dispatch_kernel(pad_start_ref, pad_len_ref, nvalid_ref, dest_ref, xp_ref, wsg_ref, wsu_ref, wsd_ref,
                     xs_out, sh_ref, zrows, sem, pad_sem, *, tm):
    ps = PACK_SUBLANES
    bf16 = jnp.bfloat16

    @pl.when(pl.program_id(0) == 0)
    def _():
        zrows[...] = jnp.zeros_like(zrows)
        bits = [1 << s for s in reversed(range(MOE_ROWS.bit_length() - 1))]

        def fill(e, carry, *, wait):
            off = pad_start_ref[e]
            n = pad_len_ref[e]
            for bit in bits:
                has = (n & bit) != 0

                @pl.when(has)
                def _(off=off, bit=bit):
                    cp = pltpu.make_async_copy(zrows.at[pl.ds(0, bit * ps), :],
                                               xs_out.at[pl.ds(pl.multiple_of(off * ps, ps), bit * ps), :], pad_sem)
                    if wait:
                        cp.wait()
                    else:
                        cp.start()

                off = off + jnp.where(has, bit, 0)
            return carry

        half_rows = MOE_ROWS // 2

        def fill_tail(h, carry, *, wait):
            cp = pltpu.make_async_copy(zrows, xs_out.at[pl.ds(pl.multiple_of(h * half_rows * ps, ps), half_rows * ps), :],
                                       pad_sem)
            if wait:
                cp.wait()
            else:
                cp.start()
            return carry

        tail = (2 * nvalid_ref[0], 2 * (xs_out.shape[0] // (MOE_ROWS * ps)))
        lax.fori_loop(0, MOE_EXPERTS, functools.partial(fill, wait=False), 0)
        lax.fori_loop(*tail, functools.partial(fill_tail, wait=False), 0)
        lax.fori_loop(0, MOE_EXPERTS, functools.partial(fill, wait=True), 0)
        lax.fori_loop(*tail, functools.partial(fill_tail, wait=True), 0)

    def start_rows(k):
        for tok in range(tm):
            row = dest_ref[0, 0, k * tm + tok]
            pltpu.make_async_copy(xp_ref.at[pl.ds(tok * ps, ps), :],
                                  xs_out.at[pl.ds(pl.multiple_of(row * ps, ps), ps), :],
                                  sem).start(priority=tok % DMA_PRIORITIES)

    xb = _unpack_rows(xp_ref, tm)
    for k in range(MOE_TOPK // 2):
        start_rows(k)
    gate = jnp.dot(xb, wsg_ref[...].astype(bf16), preferred_element_type=jnp.float32)
    up = jnp.dot(xb, wsu_ref[...].astype(bf16), preferred_element_type=jnp.float32)
    act = _silu(gate) * up
    for k in range(MOE_TOPK // 2, MOE_TOPK):
        start_rows(k)
    sh_ref[...] = jnp.dot(act.astype(bf16), wsd_ref[...].astype(bf16), preferred_element_type=jnp.float32)
    for _ in range(MOE_TOPK):
        pltpu.make_async_copy(xp_ref, xs_out.at[pl.ds(0, tm * ps), :], sem).wait()


def _dispatch_shared(x1p, dest_kt, pad_start, pad_len, nvalid, n_rows, ws_gate, ws_up, ws_down, *, tm=256):
    k, t = dest_kt.shape
    ps, d = PACK_SUBLANES, D_MODEL
    nsteps = t // tm
    dest3 = dest_kt.reshape(k, nsteps, tm).transpose(1, 0, 2).reshape(nsteps, 1, k * tm)
    const = lambda i, *_: (0, 0)
    grid_spec = pltpu.PrefetchScalarGridSpec(
        num_scalar_prefetch=3,
        grid=(nsteps,),
        in_specs=[pl.BlockSpec((1, 1, k * tm), lambda i, *_: (i, 0, 0), memory_space=pltpu.SMEM),
                  pl.BlockSpec((tm * ps, LANES), lambda i, *_: (i, 0)),
                  pl.BlockSpec((d, MOE_FFN), const), pl.BlockSpec((d, MOE_FFN), const),
                  pl.BlockSpec((MOE_FFN, d), const)],
        out_specs=[pl.BlockSpec(memory_space=pl.ANY), pl.BlockSpec((tm, d), lambda i, *_: (i, 0))],
        scratch_shapes=[pltpu.VMEM((MOE_ROWS // 2 * ps, LANES), jnp.int32),
                        pltpu.SemaphoreType.DMA(()), pltpu.SemaphoreType.DMA(())],
    )
    return pl.pallas_call(
        functools.partial(_dispatch_kernel, tm=tm),
        grid_spec=grid_spec,
        out_shape=[jax.ShapeDtypeStruct((n_rows * ps, LANES), jnp.int32),
                   jax.ShapeDtypeStruct((t, d), jnp.float32)],
        compiler_params=_cparams("arbitrary"),
        name="dispatch_shared",
    )(pad_start, pad_len, nvalid, dest3, x1p, ws_gate, ws_up, ws_down)


def _expert_kernel(blk_e_ref, nvalid_ref, xs_ref, wg_ref, wu_ref, wd_ref, y_ref):
    del blk_e_ref
    i = pl.program_id(0)
    rows = MOE_ROWS
    bf16 = jnp.bfloat16

    @pl.when(i < nvalid_ref[0])
    def _():
        xb = _unpack_rows(xs_ref, rows)
        gate = jnp.dot(xb, wg_ref[...].astype(bf16), preferred_element_type=jnp.float32)
        up = jnp.dot(xb, wu_ref[...].astype(bf16), preferred_element_type=jnp.float32)
        act = _silu(gate) * up
        y = jnp.dot(act.astype(bf16), wd_ref[...].astype(bf16), preferred_element_type=jnp.float32)
        _pack_rows(y, y_ref, rows)

    @pl.when(i >= nvalid_ref[0])
    def _():
        y_ref[...] = jnp.zeros_like(y_ref)


def _routed_experts(xs, blk_expert, nvalid, wg, wu, wd):
    nb = blk_expert.shape[0]
    rows, d = MOE_ROWS, D_MODEL
    grid_spec = pltpu.PrefetchScalarGridSpec(
        num_scalar_prefetch=2,
        grid=(nb,),
        in_specs=[pl.BlockSpec((rows * PACK_SUBLANES, LANES), lambda i, be, nv: (jnp.minimum(i, nv[0] - 1), 0)),
                  pl.BlockSpec((None, d, MOE_FFN), lambda i, be, nv: (be[i], 0, 0)),
                  pl.BlockSpec((None, d, MOE_FFN), lambda i, be, nv: (be[i], 0, 0)),
                  pl.BlockSpec((None, MOE_FFN, d), lambda i, be, nv: (be[i], 0, 0))],
        out_specs=pl.BlockSpec((rows * PACK_SUBLANES, LANES), lambda i, be, nv: (i, 0)),
    )
    return pl.pallas_call(
        _expert_kernel,
        grid_spec=grid_spec,
        out_shape=jax.ShapeDtypeStruct((nb * rows * PACK_SUBLANES, LANES), jnp.int32),
        compiler_params=_cparams("parallel"),
        name="routed_experts",
    )(blk_expert, nvalid, xs, wg, wu, wd)


def _combine_kernel(dest_ref, dest_next_ref, ys_hbm, w_ref, x_ref, sh_ref, g_ref, b_ref, o_ref, ob_ref,
                    ybuf0, ybuf1, acc, sem, *, alpha, nsteps):
    i = pl.program_id(0)
    tc, ps = COMBINE_TOK, PACK_SUBLANES
    n_rows = tc * MOE_TOPK
    half = D_MODEL // 2
    f32 = jnp.float32
    bufs = (ybuf0, ybuf1)

    def row_copy(d_ref, r, slot):
        row = d_ref[0, 0, r]
        return pltpu.make_async_copy(ys_hbm.at[pl.ds(pl.multiple_of(row * ps, ps), ps), :],
                                     bufs[slot].at[pl.ds(pl.multiple_of(r * ps, ps), ps), :], sem.at[slot])

    def wait_block(slot):
        pltpu.make_async_copy(ys_hbm.at[pl.ds(0, n_rows * ps), :], bufs[slot], sem.at[slot]).wait()

    @pl.when(i == 0)
    def _():
        def body(pair, carry):
            for prio in range(DMA_PRIORITIES):
                row_copy(dest_ref, pair * DMA_PRIORITIES + prio, 0).start(priority=prio)
            return carry
        lax.fori_loop(0, n_rows // DMA_PRIORITIES, body, 0, unroll=4)

    rows_per_chunk = n_rows // ps
    for slot in range(2):
        @pl.when(lax.rem(i, 2) == slot)
        def _(slot=slot):
            wait_block(slot)
            for c in range(ps):
                for r in range(c * rows_per_chunk, (c + 1) * rows_per_chunk):
                    row_copy(dest_next_ref, r, 1 - slot).start(priority=r % DMA_PRIORITIES)
                lo_cols = slice(c * LANES, (c + 1) * LANES)
                hi_cols = slice(half + c * LANES, half + (c + 1) * LANES)
                a_lo = alpha * x_ref[:, lo_cols] + sh_ref[:, lo_cols]
                a_hi = alpha * x_ref[:, hi_cols] + sh_ref[:, hi_cols]
                for k in range(MOE_TOPK):
                    word = bufs[slot][pl.ds(k * tc * ps + c, tc, stride=ps), :]
                    wk = w_ref[:, k:k + 1]
                    a_lo = a_lo + wk * lax.bitcast_convert_type(lax.shift_left(word, 16), f32)
                    a_hi = a_hi + wk * lax.bitcast_convert_type(word & HIGH_HALF_MASK, f32)
                acc[:, lo_cols] = a_lo
                acc[:, hi_cols] = a_hi

            @pl.when(i == nsteps - 1)
            def _():
                wait_block(1 - slot)

    x2 = _layer_norm_rows(acc[...], g_ref[...], b_ref[...])
    o_ref[...] = x2
    ob_ref[...] = x2.astype(ob_ref.dtype)


def _combine_ln(ys, dest_kt, wts_tk, x1, shared, g, b, *, alpha):
    t, d = x1.shape
    tc = COMBINE_TOK
    nsteps = t // tc
    dest3 = dest_kt.reshape(MOE_TOPK, nsteps, tc).transpose(1, 0, 2).reshape(nsteps, 1, MOE_TOPK * tc)
    smem_blk = lambda f: pl.BlockSpec((1, 1, MOE_TOPK * tc), f, memory_space=pltpu.SMEM)
    row = lambda i: (i, 0)
    const = lambda i: (0, 0)
    return pl.pallas_call(
        functools.partial(_combine_kernel, alpha=alpha, nsteps=nsteps),
        grid=(nsteps,),
        in_specs=[smem_blk(lambda i: (i, 0, 0)),
                  smem_blk(lambda i: (jnp.minimum(i + 1, nsteps - 1), 0, 0)),
                  pl.BlockSpec(memory_space=pl.ANY),
                  pl.BlockSpec((tc, MOE_TOPK), row),
                  pl.BlockSpec((tc, d), row), pl.BlockSpec((tc, d), row),
                  pl.BlockSpec((1, d), const), pl.BlockSpec((1, d), const)],
        out_specs=[pl.BlockSpec((tc, d), row), pl.BlockSpec((tc, d), row)],
        out_shape=[jax.ShapeDtypeStruct((t, d), jnp.float32), jax.ShapeDtypeStruct((t, d), jnp.bfloat16)],
        scratch_shapes=[pltpu.VMEM((MOE_TOPK * tc * PACK_SUBLANES, LANES), jnp.int32),
                        pltpu.VMEM((MOE_TOPK * tc * PACK_SUBLANES, LANES), jnp.int32),
                        pltpu.VMEM((tc, d), jnp.float32),
                        pltpu.SemaphoreType.DMA((2,))],
        compiler_params=_cparams("arbitrary"),
        name="combine_ln",
    )(dest3, dest3, ys, wts_tk, x1, shared, g.reshape(1, d), b.reshape(1, d))


def _dispatch_plan(idx_kt, rank_kt, counts):
    k, t = idx_kt.shape
    e, rows = MOE_EXPERTS, MOE_ROWS
    nb = k * t // rows + e
    padded = (counts + rows - 1) // rows * rows
    pend = jnp.cumsum(padded)
    pstart = pend - padded
    blk_start = jnp.arange(nb, dtype=jnp.int32) * rows
    blk_expert = jnp.minimum(jnp.sum(pend[None, :] <= blk_start[:, None], axis=1), e - 1).astype(jnp.int32)
    nvalid = (pend[-1] // rows).astype(jnp.int32).reshape(1)
    onehot = idx_kt[:, :, None] == jnp.arange(e, dtype=jnp.int32)
    dest = rank_kt + jnp.sum(jnp.where(onehot, pstart, 0), axis=-1).astype(jnp.int32)
    pad_start = (pstart + counts).astype(jnp.int32)
    pad_len = (padded - counts).astype(jnp.int32)
    return blk_expert, nvalid, dest, pad_start, pad_len, nb


def kernel(x, w_in, conv_a_w, conv_a_b, ln_a_g, ln_a_b, conv_c_w, w_branch, w_out, ln1_g, ln1_b, router_w,
           router_bias, w_gate, w_up, w_down, ws_gate, ws_up, ws_down, ln2_g, ln2_b):
    b, s, d = x.shape
    depth = w_in.shape[0]
    t = b * s
    alpha = (DEPTH_ALPHA_BASE * depth) ** 0.25
    bf16 = jnp.bfloat16
    xf = x.reshape(t, d)
    xb = xf.astype(bf16)
    for l in range(depth):
        u = _matmul(xb, w_in[l].astype(bf16), tm=min(t, 1024), tn=1024, out_dtype=bf16)
        u3 = u.reshape(b, s, N_IN)
        a, c = _conv_branches(u3, conv_a_w[l], conv_a_b[l], ln_a_g[l], ln_a_b[l], conv_c_w[l])
        att = _attention(u3)
        merged = _branch_merge(a.reshape(t, BRANCH_W), att.reshape(t, BRANCH_W), c.reshape(t, BRANCH_W), u,
                               w_branch[l].astype(bf16))
        x1, x1p = _outproj_ln(merged, w_out[l].astype(bf16), xf, ln1_g[l], ln1_b[l], alpha=alpha)
        idx_kt, wts_kt, rank_kt, counts = _router(x1, router_w[l], router_bias[l])
        blk_expert, nvalid, dest, pad_start, pad_len, nb = _dispatch_plan(idx_kt, rank_kt,
                                                                          counts[:, 0].astype(jnp.int32))
        xs, shared = _dispatch_shared(x1p, dest, pad_start, pad_len, nvalid, nb * MOE_ROWS,
                                      ws_gate[l], ws_up[l], ws_down[l])
        ys = _routed_experts(xs, blk_expert, nvalid, w_gate[l], w_up[l], w_down[l])
        xf, xb = _combine_ln(ys, dest, wts_kt.T, x1, shared, ln2_g[l], ln2_b[l], alpha=alpha)
    return xf.reshape(b, s, d)
```

```python
import functools

import jax
import jax.numpy as jnp
from jax import lax
from jax.experimental import pallas as pl
from jax.experimental.pallas import tpu as pltpu

D_MODEL = 2048
BRANCH_W = D_MODEL // 2
N_BRANCHES = 3
CONV_A_WIDTH = 31
CONV_C_WIDTH = 3
ATT_DILATIONS = (1, 4, 16)
ATT_BLOCK = 128
ATT_HEADS = 8
ATT_HEAD_DIM = BRANCH_W // ATT_HEADS
ATT_W = len(ATT_DILATIONS) * BRANCH_W
OFF_Q = 2 * BRANCH_W
OFF_K = OFF_Q + ATT_W
OFF_V = OFF_K + ATT_W
OFF_C = OFF_V + ATT_W
OFF_G = OFF_C + 3 * BRANCH_W
N_IN = OFF_G + N_BRANCHES * D_MODEL
MOE_EXPERTS = 64
MOE_TOPK = 8
MOE_GROUPS = 8
MOE_GROUP_SIZE = MOE_EXPERTS // MOE_GROUPS
MOE_TOPK_GROUPS = 4
MOE_FFN = D_MODEL // 8
MOE_ROUTED_SCALE = 2.5
LN_EPS = 1e-5
NEG_INF = -1e30
DEPTH_ALPHA_BASE = 2.0

LANES = 128
VMEM_LIMIT_BYTES = 52 * 1024 * 1024
MOE_ROWS = 512
PACK_SUBLANES = D_MODEL // 2 // LANES
SUBLANES = 8
HIGH_HALF_MASK = -65536
CONV_TS = 256
CONV_HALO = 32
CONV_CHUNK = 32
COMBINE_TOK = 128
DMA_PRIORITIES = 2


def _cparams(*sem):
    return pltpu.CompilerParams(dimension_semantics=sem, vmem_limit_bytes=VMEM_LIMIT_BYTES)


def _sigmoid(v):
    return 1.0 / (1.0 + jnp.exp(-v))


def _silu(v):
    return v * _sigmoid(v)


def _layer_norm_rows(v, g, b):
    mu = jnp.mean(v, axis=-1, keepdims=True)
    vc = v - mu
    var = jnp.mean(vc * vc, axis=-1, keepdims=True)
    return vc * lax.rsqrt(var + LN_EPS) * g + b


def _in_proj_kernel(x_ref, w_ref, o_ref, w_bf):
    @pl.when(pl.program_id(1) == 0)
    def _():
        w_bf[...] = w_ref[...].astype(jnp.bfloat16)

    o_ref[...] = jnp.dot(x_ref[...], w_bf[...], preferred_element_type=jnp.float32).astype(o_ref.dtype)


def _in_proj(x, w_in, layer, *, tm, tn):
    m, k = x.shape
    n = w_in.shape[-1]
    return pl.pallas_call(
        _in_proj_kernel,
        grid=(n // tn, m // tm),
        in_specs=[pl.BlockSpec((tm, k), lambda j, i: (i, 0)),
                  pl.BlockSpec((None, k, tn), lambda j, i: (layer, 0, j))],
        out_specs=pl.BlockSpec((tm, tn), lambda j, i: (i, j)),
        out_shape=jax.ShapeDtypeStruct((m, n), jnp.bfloat16),
        scratch_shapes=[pltpu.VMEM((k, tn), jnp.bfloat16)],
        compiler_params=_cparams("parallel", "arbitrary"),
        name="in_proj",
    )(x, w_in)


def _conv_kernel(val_ref, gate_ref, hval_ref, hgate_ref, cb_ref, cc_ref, ch_ref, hcc_ref, hch_ref,
                 wa_ref, ba_ref, ga_ref, bta_ref, wc_ref, a_ref, c_ref, abuf, cbuf, shifted):
    first = pl.program_id(1) == 0
    f32 = jnp.float32
    glu_h = hval_ref[0].astype(f32) * _sigmoid(hgate_ref[0].astype(f32))
    abuf[0:CONV_HALO, :] = jnp.where(first, 0.0, glu_h)
    abuf[CONV_HALO:, :] = val_ref[0].astype(f32) * _sigmoid(gate_ref[0].astype(f32))
    prod_h = hcc_ref[0].astype(f32) * hch_ref[0].astype(f32)
    cbuf[0:CONV_HALO, :] = jnp.where(first, 0.0, prod_h)
    cbuf[CONV_HALO:, :] = cc_ref[0].astype(f32) * ch_ref[0].astype(f32)

    span = CONV_HALO + CONV_TS - SUBLANES
    for j in range(1, SUBLANES):
        shifted[j - 1, 0:span, :] = abuf[j:j + span, :]

    a_off = CONV_HALO - (CONV_A_WIDTH - 1)
    c_off = CONV_HALO - (CONV_C_WIDTH - 1)
    groups = CONV_CHUNK // SUBLANES
    for r0 in range(0, CONV_TS, CONV_CHUNK):
        acc = jnp.zeros((groups, SUBLANES, BRANCH_W), f32)
        for k in range(CONV_A_WIDTH):
            j = (a_off + k) % SUBLANES
            base = r0 + a_off + k - j
            rows = abuf[base:base + CONV_CHUNK, :] if j == 0 else shifted[j - 1, base:base + CONV_CHUNK, :]
            tap = wa_ref[k * SUBLANES:(k + 1) * SUBLANES, :]
            acc = acc + tap[None] * rows.reshape(groups, SUBLANES, BRANCH_W)
        acc = acc.reshape(CONV_CHUNK, BRANCH_W)
        y = _layer_norm_rows(acc + ba_ref[...], ga_ref[...], bta_ref[...])
        a_ref[0, r0:r0 + CONV_CHUNK, :] = _silu(y).astype(a_ref.dtype)
        cacc = jnp.zeros((CONV_CHUNK, BRANCH_W), f32)
        for k in range(CONV_C_WIDTH):
            cacc = cacc + wc_ref[k:k + 1, :] * cbuf[r0 + c_off + k:r0 + c_off + k + CONV_CHUNK, :]
        c_ref[0, r0:r0 + CONV_CHUNK, :] = (cb_ref[0, r0:r0 + CONV_CHUNK, :].astype(f32) * cacc).astype(c_ref.dtype)


def _conv_branches(u, conv_a_w, conv_a_b, ln_a_g, ln_a_b, conv_c_w):
    b, s, _ = u.shape
    ts, halo = CONV_TS, CONV_HALO
    w = BRANCH_W
    ratio = ts // halo

    def main(col):
        return pl.BlockSpec((1, ts, w), lambda bi, i: (bi, i, col))

    def prev(col):
        return pl.BlockSpec((1, halo, w), lambda bi, i: (bi, jnp.maximum(i * ratio - 1, 0), col))

    def full(rows):
        return pl.BlockSpec((rows, w), lambda bi, i: (0, 0))

    cblk = OFF_C // w
    out = pl.pallas_call(
        _conv_kernel,
        grid=(b, s // ts),
        in_specs=[main(0), main(1), prev(0), prev(1),
                  main(cblk), main(cblk + 1), main(cblk + 2), prev(cblk + 1), prev(cblk + 2),
                  full(CONV_A_WIDTH * SUBLANES), full(1), full(1), full(1), full(CONV_C_WIDTH)],
        out_specs=[pl.BlockSpec((1, ts, w), lambda bi, i: (bi, i, 0)),
                   pl.BlockSpec((1, ts, w), lambda bi, i: (bi, i, 0))],
        out_shape=[jax.ShapeDtypeStruct((b, s, w), jnp.bfloat16),
                   jax.ShapeDtypeStruct((b, s, w), jnp.bfloat16)],
        scratch_shapes=[pltpu.VMEM((halo + ts, w), jnp.float32),
                        pltpu.VMEM((halo + ts, w), jnp.float32),
                        pltpu.VMEM((SUBLANES - 1, halo + ts, w), jnp.float32)],
        compiler_params=_cparams("parallel", "parallel"),
        name="conv_branches",
    )(u, u, u, u, u, u, u, u, u,
      jnp.repeat(conv_a_w, SUBLANES, axis=0), conv_a_b.reshape(1, w), ln_a_g.reshape(1, w), ln_a_b.reshape(1, w),
      conv_c_w)
    return out


def _attn_kernel(q0, q1, q2, k0, k1, k2, v0, v1, v2, o_ref, q32, k32, v32, nbuf, mbuf, zbuf, *, seq):
    f32, bf16 = jnp.float32, jnp.bfloat16
    blk = ATT_BLOCK
    scale = ATT_HEAD_DIM ** -0.5
    qi = lax.broadcasted_iota(jnp.int32, (blk, 2 * blk), 0)
    ki = lax.broadcasted_iota(jnp.int32, (blk, 2 * blk), 1)
    dist = blk + qi - ki
    valid2 = (dist >= 0) & (dist <= blk)
    valid1 = (lax.broadcasted_iota(jnp.int32, (blk, blk), 0)
              >= lax.broadcasted_iota(jnp.int32, (blk, blk), 1))

    for g, (r, qr, kr, vr) in enumerate(zip(ATT_DILATIONS, (q0, q1, q2), (k0, k1, k2), (v0, v1, v2))):
        length = seq // r
        nblk = length // blk
        if r > 1:
            q32[...] = qr[0].astype(f32)
            k32[...] = kr[0].astype(f32)
            v32[...] = vr[0].astype(f32)
        for c in range(r):
            for n in range(nblk):
                q_start = c + n * blk * r
                k_start = c + (n - 1) * blk * r if n > 0 else q_start
                k_len = 2 * blk if n > 0 else blk
                if r == 1:
                    q = qr[0, q_start:q_start + blk, :]
                    k = kr[0, k_start:k_start + k_len, :]
                    v = vr[0, k_start:k_start + k_len, :]
                else:
                    q = q32[pl.ds(q_start, blk, stride=r), :].astype(bf16)
                    k = k32[pl.ds(k_start, k_len, stride=r), :].astype(bf16)
                    v = v32[pl.ds(k_start, k_len, stride=r), :].astype(bf16)
                s = lax.dot_general(q, k, (((1,), (1,)), ((), ())), preferred_element_type=f32) * scale
                s = jnp.where(valid2 if n > 0 else valid1, s, NEG_INF)
                m = jnp.max(s, axis=-1, keepdims=True)
                p = jnp.exp(s - m)
                z = jnp.sum(p, axis=-1, keepdims=True)
                num = jnp.dot(p.astype(bf16), v, preferred_element_type=f32)
                if r == 1:
                    rows = pl.ds(q_start, blk)
                else:
                    rows = pl.ds(q_start, blk, stride=r)
                nbuf[g, rows, :] = num
                mbuf[g, rows, :] = jnp.broadcast_to(m, (blk, ATT_HEAD_DIM))
                zbuf[g, rows, :] = jnp.broadcast_to(z, (blk, ATT_HEAD_DIM))

    rows_per = 256
    for r0 in range(0, seq, rows_per):
        sl = slice(r0, r0 + rows_per)
        m_all = jnp.maximum(jnp.maximum(mbuf[0, sl, :], mbuf[1, sl, :]), mbuf[2, sl, :])
        num = jnp.zeros((rows_per, ATT_HEAD_DIM), f32)
        den = jnp.zeros((rows_per, ATT_HEAD_DIM), f32)
        for g in range(len(ATT_DILATIONS)):
            e = jnp.exp(mbuf[g, sl, :] - m_all)
            num = num + e * nbuf[g, sl, :]
            den = den + e * zbuf[g, sl, :]
        o_ref[0, sl, :] = (num / den).astype(o_ref.dtype)


def _attention(u):
    b, s, _ = u.shape
    dh = ATT_HEAD_DIM
    n_groups = len(ATT_DILATIONS)

    def spec(off, g):
        base = (off + g * BRANCH_W) // dh
        return pl.BlockSpec((1, s, dh), lambda bi, h: (bi, 0, base + h))

    in_specs = ([spec(OFF_Q, g) for g in range(n_groups)] + [spec(OFF_K, g) for g in range(n_groups)]
                + [spec(OFF_V, g) for g in range(n_groups)])
    return pl.pallas_call(
        functools.partial(_attn_kernel, seq=s),
        grid=(b, ATT_HEADS),
        in_specs=in_specs,
        out_specs=pl.BlockSpec((1, s, dh), lambda bi, h: (bi, 0, h)),
        out_shape=jax.ShapeDtypeStruct((b, s, BRANCH_W), jnp.bfloat16),
        scratch_shapes=[pltpu.VMEM((s, dh), jnp.float32)] * 3
        + [pltpu.VMEM((n_groups, s, dh), jnp.float32)] * 3,
        compiler_params=_cparams("parallel", "parallel"),
        name="dilated_attention",
    )(*([u] * 9))


def _merge_kernel(a_ref, att_ref, c_ref, g0_ref, g1_ref, g2_ref, wb_ref, o_ref):
    f32 = jnp.float32
    acc = None
    for n, (br, gr) in enumerate(((a_ref, g0_ref), (att_ref, g1_ref), (c_ref, g2_ref))):
        proj = jnp.dot(br[...], wb_ref[n], preferred_element_type=f32)
        term = _sigmoid(gr[...].astype(f32)) * proj
        acc = term if acc is None else acc + term
    o_ref[...] = acc.astype(o_ref.dtype)


def _branch_merge(a, att, c, u2d, w_branch, *, tm=512, tn=1024):
    t, w = a.shape
    d = D_MODEL
    gblk = OFF_G // tn

    def gate(n):
        return pl.BlockSpec((tm, tn), lambda j, i: (i, gblk + n * (d // tn) + j))

    row = pl.BlockSpec((tm, w), lambda j, i: (i, 0))
    return pl.pallas_call(
        _merge_kernel,
        grid=(d // tn, t // tm),
        in_specs=[row, row, row, gate(0), gate(1), gate(2),
                  pl.BlockSpec((N_BRANCHES, w, tn), lambda j, i: (0, 0, j))],
        out_specs=pl.BlockSpec((tm, tn), lambda j, i: (i, j)),
        out_shape=jax.ShapeDtypeStruct((t, d), jnp.bfloat16),
        compiler_params=_cparams("parallel", "parallel"),
        name="branch_merge",
    )(a, att, c, u2d, u2d, u2d, w_branch)


def _pack_rows(y, out_ref, rows):
    half = D_MODEL // 2
    lo = lax.bitcast_convert_type(y[:, :half].astype(jnp.bfloat16).astype(jnp.float32), jnp.int32)
    hi = lax.bitcast_convert_type(y[:, half:].astype(jnp.bfloat16).astype(jnp.float32), jnp.int32)
    word = (hi & HIGH_HALF_MASK) | lax.shift_right_logical(lo, 16)
    for c in range(PACK_SUBLANES):
        out_ref[pl.ds(c, rows, stride=PACK_SUBLANES), :] = word[:, c * LANES:(c + 1) * LANES]


def _unpack_rows(in_ref, rows):
    los, his = [], []
    for c in range(PACK_SUBLANES):
        word = in_ref[pl.ds(c, rows, stride=PACK_SUBLANES), :]
        los.append(lax.bitcast_convert_type(lax.shift_left(word, 16), jnp.float32).astype(jnp.bfloat16))
        his.append(lax.bitcast_convert_type(word & HIGH_HALF_MASK, jnp.float32).astype(jnp.bfloat16))
    return jnp.concatenate(los + his, axis=1)


def _outproj_ln_kernel(m_ref, w_ref, x_ref, g_ref, b_ref, o_ref, p_ref, *, alpha):
    y = jnp.dot(m_ref[...], w_ref[...], preferred_element_type=jnp.float32)
    x1 = _layer_norm_rows(alpha * x_ref[...] + y, g_ref[...], b_ref[...])
    o_ref[...] = x1
    _pack_rows(x1, p_ref, x1.shape[0])


def _outproj_ln(merged, w_out, x, g, b, *, alpha, tm=512):
    t, d = x.shape
    row = lambda i: (i, 0)
    const = lambda i: (0, 0)
    return pl.pallas_call(
        functools.partial(_outproj_ln_kernel, alpha=alpha),
        grid=(t // tm,),
        in_specs=[pl.BlockSpec((tm, d), row), pl.BlockSpec((d, d), const), pl.BlockSpec((tm, d), row),
                  pl.BlockSpec((1, d), const), pl.BlockSpec((1, d), const)],
        out_specs=[pl.BlockSpec((tm, d), row), pl.BlockSpec((tm * PACK_SUBLANES, LANES), row)],
        out_shape=[jax.ShapeDtypeStruct((t, d), jnp.float32),
                   jax.ShapeDtypeStruct((t * PACK_SUBLANES, LANES), jnp.int32)],
        compiler_params=_cparams("parallel"),
        name="out_proj_ln",
    )(merged, w_out, x, g.reshape(1, d), b.reshape(1, d))


def _first_argmax(vals, iota, axis, size):
    mx = jnp.max(vals, axis=axis, keepdims=True)
    idx = jnp.min(jnp.where(vals == mx, iota, size), axis=axis, keepdims=True)
    return mx, idx


def _router_kernel(x_ref, wt_ref, bias_ref, idx_ref, wts_ref, rank_ref, cnt_ref, carry):
    f32 = jnp.float32
    e, gsz, ng = MOE_EXPERTS, MOE_GROUP_SIZE, MOE_GROUPS
    tm = x_ref.shape[0]

    @pl.when(pl.program_id(0) == 0)
    def _():
        carry[...] = jnp.zeros_like(carry)

    logits = lax.dot_general(wt_ref[...], x_ref[...], (((1,), (1,)), ((), ())),
                             precision=lax.Precision.HIGHEST, preferred_element_type=f32)
    scores = _sigmoid(logits)
    choice = scores + bias_ref[...]
    grp = choice.reshape(ng, gsz, tm)
    iota_in = lax.broadcasted_iota(jnp.int32, (ng, gsz, tm), 1)
    m1, i1 = _first_argmax(grp, iota_in, 1, gsz)
    m2 = jnp.max(jnp.where(iota_in == i1, -jnp.inf, grp), axis=1, keepdims=True)
    grp_score = (m1 + m2).reshape(ng, tm)
    iota_g = lax.broadcasted_iota(jnp.int32, (ng, tm), 0)
    grp_mask = jnp.zeros((ng, tm), jnp.bool_)
    for _ in range(MOE_TOPK_GROUPS):
        _, gi = _first_argmax(grp_score, iota_g, 0, ng)
        hit = iota_g == gi
        grp_mask = grp_mask | hit
        grp_score = jnp.where(hit, -jnp.inf, grp_score)
    expert_mask = jnp.broadcast_to(grp_mask.reshape(ng, 1, tm), (ng, gsz, tm)).reshape(e, tm)
    masked = jnp.where(expert_mask, choice, NEG_INF)
    iota_e = lax.broadcasted_iota(jnp.int32, (e, tm), 0)
    idxs, ws, hits = [], [], []
    for _ in range(MOE_TOPK):
        _, ei = _first_argmax(masked, iota_e, 0, e)
        hit = iota_e == ei
        idxs.append(ei)
        hits.append(hit)
        ws.append(jnp.sum(jnp.where(hit, scores, 0.0), axis=0, keepdims=True))
        masked = jnp.where(hit, -jnp.inf, masked)
    w_all = jnp.concatenate(ws, axis=0)
    idx_ref[...] = jnp.concatenate(idxs, axis=0)
    wts_ref[...] = w_all / jnp.sum(w_all, axis=0, keepdims=True) * MOE_ROUTED_SCALE

    chosen = hits[0]
    for hit in hits[1:]:
        chosen = chosen | hit
    chosen_f = jnp.where(chosen, 1.0, 0.0)
    earlier = (lax.broadcasted_iota(jnp.int32, (tm, tm), 0) < lax.broadcasted_iota(jnp.int32, (tm, tm), 1))
    prefix = jnp.dot(chosen_f.astype(jnp.bfloat16), jnp.where(earlier, 1.0, 0.0).astype(jnp.bfloat16),
                     preferred_element_type=f32)
    base = carry[:, 0:1] + prefix
    ranks = [jnp.sum(jnp.where(hit, base, 0.0), axis=0, keepdims=True) for hit in hits]
    rank_ref[...] = jnp.concatenate(ranks, axis=0).astype(jnp.int32)
    carry[...] = carry[...] + jnp.sum(chosen_f, axis=1, keepdims=True)
    cnt_ref[...] = carry[...]


def _router(x1, router_w, router_bias, *, tm=512):
    t, d = x1.shape
    e = MOE_EXPERTS
    tok = pl.BlockSpec((MOE_TOPK, tm), lambda i: (0, i))
    return pl.pallas_call(
        _router_kernel,
        grid=(t // tm,),
        in_specs=[pl.BlockSpec((tm, d), lambda i: (i, 0)),
                  pl.BlockSpec((e, d), lambda i: (0, 0)),
                  pl.BlockSpec((e, 1), lambda i: (0, 0))],
        out_specs=[tok, tok, tok, pl.BlockSpec((e, LANES), lambda i: (0, 0))],
        out_shape=[jax.ShapeDtypeStruct((MOE_TOPK, t), jnp.int32),
                   jax.ShapeDtypeStruct((MOE_TOPK, t), jnp.float32),
                   jax.ShapeDtypeStruct((MOE_TOPK, t), jnp.int32),
                   jax.ShapeDtypeStruct((e, LANES), jnp.float32)],
        scratch_shapes=[pltpu.VMEM((e, LANES), jnp.float32)],
        compiler_params=_cparams("arbitrary"),
        name="router_topk",
    )(x1, router_w.T, router_bias.reshape(e, 1).astype(jnp.float32))


def _dispatch_kernel(pad_start_ref, pad_len_ref, nvalid_ref, dest_ref, xp_ref, wsg_ref, wsu_ref, wsd_ref,
                     xs_out, sh_ref, zrows, sem, pad_sem, *, tm):
    ps = PACK_SUBLANES
    bf16 = jnp.bfloat16

    @pl.when(pl.program_id(0) == 0)
    def _():
        zrows[...] = jnp.zeros_like(zrows)
        bits = [1 << s for s in reversed(range(MOE_ROWS.bit_length() - 1))]

        def fill(e, carry, *, wait):
            off = pad_start_ref[e]
            n = pad_len_ref[e]
            for bit in bits:
                has = (n & bit) != 0

                @pl.when(has)
                def _(off=off, bit=bit):
                    cp = pltpu.make_async_copy(zrows.at[pl.ds(0, bit * ps), :],
                                               xs_out.at[pl.ds(pl.multiple_of(off * ps, ps), bit * ps), :], pad_sem)
                    if wait:
                        cp.wait()
                    else:
                        cp.start()

                off = off + jnp.where(has, bit, 0)
            return carry

        half_rows = MOE_ROWS // 2

        def fill_tail(h, carry, *, wait):
            cp = pltpu.make_async_copy(zrows, xs_out.at[pl.ds(pl.multiple_of(h * half_rows * ps, ps), half_rows * ps), :],
                                       pad_sem)
            if wait:
                cp.wait()
            else:
                cp.start()
            return carry

        tail = (2 * nvalid_ref[0], 2 * (xs_out.shape[0] // (MOE_ROWS * ps)))
        lax.fori_loop(0, MOE_EXPERTS, functools.partial(fill, wait=False), 0)
        lax.fori_loop(*tail, functools.partial(fill_tail, wait=False), 0)
        lax.fori_loop(0, MOE_EXPERTS, functools.partial(fill, wait=True), 0)
        lax.fori_loop(*tail, functools.partial(fill_tail, wait=True), 0)

    def start_rows(k):
        for tok in range(tm):
            row = dest_ref[0, 0, k * tm + tok]
            pltpu.make_async_copy(xp_ref.at[pl.ds(tok * ps, ps), :],
                                  xs_out.at[pl.ds(pl.multiple_of(row * ps, ps), ps), :],
                                  sem).start(priority=tok % DMA_PRIORITIES)

    xb = _unpack_rows(xp_ref, tm)
    for k in range(MOE_TOPK // 2):
        start_rows(k)
    gate = jnp.dot(xb, wsg_ref[...].astype(bf16), preferred_element_type=jnp.float32)
    up = jnp.dot(xb, wsu_ref[...].astype(bf16), preferred_element_type=jnp.float32)
    act = _silu(gate) * up
    for k in range(MOE_TOPK // 2, MOE_TOPK):
        start_rows(k)
    sh_ref[...] = jnp.dot(act.astype(bf16), wsd_ref[...].astype(bf16), preferred_element_type=jnp.float32)
    for _ in range(MOE_TOPK):
        pltpu.make_async_copy(xp_ref, xs_out.at[pl.ds(0, tm * ps), :], sem).wait()


def _dispatch_shared(x1p, dest_kt, pad_start, pad_len, nvalid, n_rows, ws_gate, ws_up, ws_down, layer, *, tm=256):
    k, t = dest_kt.shape
    ps, d = PACK_SUBLANES, D_MODEL
    nsteps = t // tm
    dest3 = dest_kt.reshape(k, nsteps, tm).transpose(1, 0, 2).reshape(nsteps, 1, k * tm)
    wspec = lambda shape: pl.BlockSpec((None,) + shape, lambda i, *_: (layer, 0, 0))
    grid_spec = pltpu.PrefetchScalarGridSpec(
        num_scalar_prefetch=3,
        grid=(nsteps,),
        in_specs=[pl.BlockSpec((1, 1, k * tm), lambda i, *_: (i, 0, 0), memory_space=pltpu.SMEM),
                  pl.BlockSpec((tm * ps, LANES), lambda i, *_: (i, 0)),
                  wspec((d, MOE_FFN)), wspec((d, MOE_FFN)), wspec((MOE_FFN, d))],
        out_specs=[pl.BlockSpec(memory_space=pl.ANY), pl.BlockSpec((tm, d), lambda i, *_: (i, 0))],
        scratch_shapes=[pltpu.VMEM((MOE_ROWS // 2 * ps, LANES), jnp.int32),
                        pltpu.SemaphoreType.DMA(()), pltpu.SemaphoreType.DMA(())],
    )
    return pl.pallas_call(
        functools.partial(_dispatch_kernel, tm=tm),
        grid_spec=grid_spec,
        out_shape=[jax.ShapeDtypeStruct((n_rows * ps, LANES), jnp.int32),
                   jax.ShapeDtypeStruct((t, d), jnp.float32)],
        compiler_params=_cparams("arbitrary"),
        name="dispatch_shared",
    )(pad_start, pad_len, nvalid, dest3, x1p, ws_gate, ws_up, ws_down)


def _expert_kernel(blk_e_ref, nvalid_ref, xs_ref, wg_ref, wu_ref, wd_ref, y_ref, wg_bf, wu_bf, wd_bf):
    i = pl.program_id(0)
    rows = MOE_ROWS
    bf16 = jnp.bfloat16

    @pl.when((i == 0) | (blk_e_ref[i] != blk_e_ref[jnp.maximum(i - 1, 0)]))
    def _():
        wg_bf[...] = wg_ref[...].astype(bf16)
        wu_bf[...] = wu_ref[...].astype(bf16)
        wd_bf[...] = wd_ref[...].astype(bf16)

    @pl.when(i < nvalid_ref[0])
    def _():
        xb = _unpack_rows(xs_ref, rows)
        gate = jnp.dot(xb, wg_bf[...], preferred_element_type=jnp.float32)
        up = jnp.dot(xb, wu_bf[...], preferred_element_type=jnp.float32)
        act = _silu(gate) * up
        y = jnp.dot(act.astype(bf16), wd_bf[...], preferred_element_type=jnp.float32)
        _pack_rows(y, y_ref, rows)

    @pl.when(i >= nvalid_ref[0])
    def _():
        y_ref[...] = jnp.zeros_like(y_ref)


def _routed_experts(xs, blk_expert, nvalid, w_gate, w_up, w_down, layer):
    nb = blk_expert.shape[0]
    rows, d = MOE_ROWS, D_MODEL
    wspec = lambda shape: pl.BlockSpec((None, None) + shape, lambda i, be, nv: (layer, be[i], 0, 0))
    grid_spec = pltpu.PrefetchScalarGridSpec(
        num_scalar_prefetch=2,
        grid=(nb,),
        in_specs=[pl.BlockSpec((rows * PACK_SUBLANES, LANES), lambda i, be, nv: (jnp.minimum(i, nv[0] - 1), 0)),
                  wspec((d, MOE_FFN)), wspec((d, MOE_FFN)), wspec((MOE_FFN, d))],
        out_specs=pl.BlockSpec((rows * PACK_SUBLANES, LANES), lambda i, be, nv: (i, 0)),
        scratch_shapes=[pltpu.VMEM((d, MOE_FFN), jnp.bfloat16), pltpu.VMEM((d, MOE_FFN), jnp.bfloat16),
                        pltpu.VMEM((MOE_FFN, d), jnp.bfloat16)],
    )
    return pl.pallas_call(
        _expert_kernel,
        grid_spec=grid_spec,
        out_shape=jax.ShapeDtypeStruct((nb * rows * PACK_SUBLANES, LANES), jnp.int32),
        compiler_params=_cparams("arbitrary"),
        name="routed_experts",
    )(blk_expert, nvalid, xs, w_gate, w_up, w_down)


def _combine_kernel(dest_ref, dest_next_ref, ys_hbm, w_ref, x_ref, sh_ref, g_ref, b_ref, o_ref, ob_ref,
                    ybuf0, ybuf1, acc, sem, *, alpha, nsteps):
    i = pl.program_id(0)
    tc, ps = COMBINE_TOK, PACK_SUBLANES
    n_rows = tc * MOE_TOPK
    half = D_MODEL // 2
    f32 = jnp.float32
    bufs = (ybuf0, ybuf1)

    def row_copy(d_ref, r, slot):
        row = d_ref[0, 0, r]
        return pltpu.make_async_copy(ys_hbm.at[pl.ds(pl.multiple_of(row * ps, ps), ps), :],
                                     bufs[slot].at[pl.ds(pl.multiple_of(r * ps, ps), ps), :], sem.at[slot])

    def wait_block(slot):
        pltpu.make_async_copy(ys_hbm.at[pl.ds(0, n_rows * ps), :], bufs[slot], sem.at[slot]).wait()

    @pl.when(i == 0)
    def _():
        def body(pair, carry):
            for prio in range(DMA_PRIORITIES):
                row_copy(dest_ref, pair * DMA_PRIORITIES + prio, 0).start(priority=prio)
            return carry
        lax.fori_loop(0, n_rows // DMA_PRIORITIES, body, 0, unroll=4)

    rows_per_chunk = n_rows // ps
    for slot in range(2):
        @pl.when(lax.rem(i, 2) == slot)
        def _(slot=slot):
            wait_block(slot)
            for c in range(ps):
                for r in range(c * rows_per_chunk, (c + 1) * rows_per_chunk):
                    row_copy(dest_next_ref, r, 1 - slot).start(priority=r % DMA_PRIORITIES)
                lo_cols = slice(c * LANES, (c + 1) * LANES)
                hi_cols = slice(half + c * LANES, half + (c + 1) * LANES)
                a_lo = alpha * x_ref[:, lo_cols] + sh_ref[:, lo_cols]
                a_hi = alpha * x_ref[:, hi_cols] + sh_ref[:, hi_cols]
                for k in range(MOE_TOPK):
                    word = bufs[slot][pl.ds(k * tc * ps + c, tc, stride=ps), :]
                    wk = w_ref[:, k:k + 1]
                    a_lo = a_lo + wk * lax.bitcast_convert_type(lax.shift_left(word, 16), f32)
                    a_hi = a_hi + wk * lax.bitcast_convert_type(word & HIGH_HALF_MASK, f32)
                acc[:, lo_cols] = a_lo
                acc[:, hi_cols] = a_hi

            @pl.when(i == nsteps - 1)
            def _():
                wait_block(1 - slot)

    x2 = _layer_norm_rows(acc[...], g_ref[...], b_ref[...])
    o_ref[...] = x2
    ob_ref[...] = x2.astype(ob_ref.dtype)


def _combine_ln(ys, dest_kt, wts_tk, x1, shared, g, b, *, alpha):
    t, d = x1.shape
    tc = COMBINE_TOK
    nsteps = t // tc
    dest3 = dest_kt.reshape(MOE_TOPK, nsteps, tc).transpose(1, 0, 2).reshape(nsteps, 1, MOE_TOPK * tc)
    smem_blk = lambda f: pl.BlockSpec((1, 1, MOE_TOPK * tc), f, memory_space=pltpu.SMEM)
    row = lambda i: (i, 0)
    const = lambda i: (0, 0)
    return pl.pallas_call(
        functools.partial(_combine_kernel, alpha=alpha, nsteps=nsteps),
        grid=(nsteps,),
        in_specs=[smem_blk(lambda i: (i, 0, 0)),
                  smem_blk(lambda i: (jnp.minimum(i + 1, nsteps - 1), 0, 0)),
                  pl.BlockSpec(memory_space=pl.ANY),
                  pl.BlockSpec((tc, MOE_TOPK), row),
                  pl.BlockSpec((tc, d), row), pl.BlockSpec((tc, d), row),
                  pl.BlockSpec((1, d), const), pl.BlockSpec((1, d), const)],
        out_specs=[pl.BlockSpec((tc, d), row), pl.BlockSpec((tc, d), row)],
        out_shape=[jax.ShapeDtypeStruct((t, d), jnp.float32), jax.ShapeDtypeStruct((t, d), jnp.bfloat16)],
        scratch_shapes=[pltpu.VMEM((MOE_TOPK * tc * PACK_SUBLANES, LANES), jnp.int32),
                        pltpu.VMEM((MOE_TOPK * tc * PACK_SUBLANES, LANES), jnp.int32),
                        pltpu.VMEM((tc, d), jnp.float32),
                        pltpu.SemaphoreType.DMA((2,))],
        compiler_params=_cparams("arbitrary"),
        name="combine_ln",
    )(dest3, dest3, ys, wts_tk, x1, shared, g.reshape(1, d), b.reshape(1, d))


def _dispatch_plan(idx_kt, rank_kt, counts):
    k, t = idx_kt.shape
    e, rows = MOE_EXPERTS, MOE_ROWS
    nb = k * t // rows + e
    padded = (counts + rows - 1) // rows * rows
    pend = jnp.cumsum(padded)
    pstart = pend - padded
    blk_start = jnp.arange(nb, dtype=jnp.int32) * rows
    blk_expert = jnp.minimum(jnp.sum(pend[None, :] <= blk_start[:, None], axis=1), e - 1).astype(jnp.int32)
    nvalid = (pend[-1] // rows).astype(jnp.int32).reshape(1)
    onehot = idx_kt[:, :, None] == jnp.arange(e, dtype=jnp.int32)
    dest = rank_kt + jnp.sum(jnp.where(onehot, pstart, 0), axis=-1).astype(jnp.int32)
    pad_start = (pstart + counts).astype(jnp.int32)
    pad_len = (padded - counts).astype(jnp.int32)
    return blk_expert, nvalid, dest, pad_start, pad_len, nb


def kernel(x, w_in, conv_a_w, conv_a_b, ln_a_g, ln_a_b, conv_c_w, w_branch, w_out, ln1_g, ln1_b, router_w,
           router_bias, w_gate, w_up, w_down, ws_gate, ws_up, ws_down, ln2_g, ln2_b):
    b, s, d = x.shape
    depth = w_in.shape[0]
    t = b * s
    alpha = (DEPTH_ALPHA_BASE * depth) ** 0.25
    bf16 = jnp.bfloat16
    xf = x.reshape(t, d)
    xb = xf.astype(bf16)
    for l in range(depth):
        u = _in_proj(xb, w_in, l, tm=min(t, 1024), tn=1024)
        u3 = u.reshape(b, s, N_IN)
        a, c = _conv_branches(u3, conv_a_w[l], conv_a_b[l], ln_a_g[l], ln_a_b[l], conv_c_w[l])
        att = _attention(u3)
        merged = _branch_merge(a.reshape(t, BRANCH_W), att.reshape(t, BRANCH_W), c.reshape(t, BRANCH_W), u,
                               w_branch[l].astype(bf16))
        x1, x1p = _outproj_ln(merged, w_out[l].astype(bf16), xf, ln1_g[l], ln1_b[l], alpha=alpha)
        idx_kt, wts_kt, rank_kt, counts = _router(x1, router_w[l], router_bias[l])
        blk_expert, nvalid, dest, pad_start, pad_len, nb = _dispatch_plan(idx_kt, rank_kt,
                                                                          counts[:, 0].astype(jnp.int32))
        xs, shared = _dispatch_shared(x1p, dest, pad_start, pad_len, nvalid, nb * MOE_ROWS,
                                      ws_gate, ws_up, ws_down, l)
        ys = _routed_experts(xs, blk_expert, nvalid, w_gate, w_up, w_down, l)
        xf, xb = _combine_ln(ys, dest, wts_kt.T, x1, shared, ln2_g[l], ln2_b[l], alpha=alpha)
    return xf.reshape(b, s, d)
```

```python
import functools

import jax
import jax.numpy as jnp
from jax import lax
from jax.experimental import pallas as pl
from jax.experimental.pallas import tpu as pltpu

D_MODEL = 2048
BRANCH_W = D_MODEL // 2
N_BRANCHES = 3
CONV_A_WIDTH = 31
CONV_C_WIDTH = 3
ATT_DILATIONS = (1, 4, 16)
ATT_BLOCK = 128
ATT_HEADS = 8
ATT_HEAD_DIM = BRANCH_W // ATT_HEADS
ATT_W = len(ATT_DILATIONS) * BRANCH_W
OFF_Q = 2 * BRANCH_W
OFF_K = OFF_Q + ATT_W
OFF_V = OFF_K + ATT_W
OFF_C = OFF_V + ATT_W
OFF_G = OFF_C + 3 * BRANCH_W
N_IN = OFF_G + N_BRANCHES * D_MODEL
MOE_EXPERTS = 64
MOE_TOPK = 8
MOE_GROUPS = 8
MOE_GROUP_SIZE = MOE_EXPERTS // MOE_GROUPS
MOE_TOPK_GROUPS = 4
MOE_FFN = D_MODEL // 8
MOE_ROUTED_SCALE = 2.5
LN_EPS = 1e-5
NEG_INF = -1e30
DEPTH_ALPHA_BASE = 2.0

LANES = 128
VMEM_LIMIT_BYTES = 52 * 1024 * 1024
MOE_ROWS = 512
PACK_SUBLANES = D_MODEL // 2 // LANES
SUBLANES = 8
HIGH_HALF_MASK = -65536
CONV_TS = 512
CONV_HALO = 32
CONV_CHUNK = 32
COMBINE_TOK = 128
DMA_PRIORITIES = 2


def _cparams(*sem):
    return pltpu.CompilerParams(dimension_semantics=sem, vmem_limit_bytes=VMEM_LIMIT_BYTES)


def _sigmoid(v):
    return 1.0 / (1.0 + jnp.exp(-v))


def _silu(v):
    return v * _sigmoid(v)


def _layer_norm_rows(v, g, b):
    mu = jnp.mean(v, axis=-1, keepdims=True)
    vc = v - mu
    var = jnp.mean(vc * vc, axis=-1, keepdims=True)
    return vc * lax.rsqrt(var + LN_EPS) * g + b


def _in_proj_kernel(x_ref, w_ref, o_ref, w_bf):
    @pl.when(pl.program_id(1) == 0)
    def _():
        w_bf[...] = w_ref[...].astype(jnp.bfloat16)

    o_ref[...] = jnp.dot(x_ref[...], w_bf[...], preferred_element_type=jnp.float32).astype(o_ref.dtype)


def _in_proj(x, w_in, layer, *, tm, tn):
    m, k = x.shape
    n = w_in.shape[-1]
    return pl.pallas_call(
        _in_proj_kernel,
        grid=(n // tn, m // tm),
        in_specs=[pl.BlockSpec((tm, k), lambda j, i: (i, 0)),
                  pl.BlockSpec((None, k, tn), lambda j, i: (layer, 0, j))],
        out_specs=pl.BlockSpec((tm, tn), lambda j, i: (i, j)),
        out_shape=jax.ShapeDtypeStruct((m, n), jnp.bfloat16),
        scratch_shapes=[pltpu.VMEM((k, tn), jnp.bfloat16)],
        compiler_params=_cparams("parallel", "arbitrary"),
        name="in_proj",
    )(x, w_in)


def _conv_kernel(val_ref, gate_ref, hval_ref, hgate_ref, cb_ref, cc_ref, ch_ref, hcc_ref, hch_ref,
                 wa_ref, ba_ref, ga_ref, bta_ref, wc_ref, a_ref, c_ref, abuf, cbuf, shifted):
    first = pl.program_id(1) == 0
    f32 = jnp.float32
    glu_h = hval_ref[0].astype(f32) * _sigmoid(hgate_ref[0].astype(f32))
    abuf[0:CONV_HALO, :] = jnp.where(first, 0.0, glu_h)
    abuf[CONV_HALO:, :] = val_ref[0].astype(f32) * _sigmoid(gate_ref[0].astype(f32))
    prod_h = hcc_ref[0].astype(f32) * hch_ref[0].astype(f32)
    cbuf[0:CONV_HALO, :] = jnp.where(first, 0.0, prod_h)
    cbuf[CONV_HALO:, :] = cc_ref[0].astype(f32) * ch_ref[0].astype(f32)

    span = CONV_HALO + CONV_TS - SUBLANES
    for j in range(1, SUBLANES):
        shifted[j - 1, 0:span, :] = abuf[j:j + span, :]

    a_off = CONV_HALO - (CONV_A_WIDTH - 1)
    c_off = CONV_HALO - (CONV_C_WIDTH - 1)
    groups = CONV_CHUNK // SUBLANES
    for r0 in range(0, CONV_TS, CONV_CHUNK):
        acc = jnp.zeros((groups, SUBLANES, BRANCH_W), f32)
        for k in range(CONV_A_WIDTH):
            j = (a_off + k) % SUBLANES
            base = r0 + a_off + k - j
            rows = abuf[base:base + CONV_CHUNK, :] if j == 0 else shifted[j - 1, base:base + CONV_CHUNK, :]
            tap = wa_ref[k * SUBLANES:(k + 1) * SUBLANES, :]
            acc = acc + tap[None] * rows.reshape(groups, SUBLANES, BRANCH_W)
        acc = acc.reshape(CONV_CHUNK, BRANCH_W)
        y = _layer_norm_rows(acc + ba_ref[...], ga_ref[...], bta_ref[...])
        a_ref[0, r0:r0 + CONV_CHUNK, :] = _silu(y).astype(a_ref.dtype)
        cacc = jnp.zeros((CONV_CHUNK, BRANCH_W), f32)
        for k in range(CONV_C_WIDTH):
            cacc = cacc + wc_ref[k:k + 1, :] * cbuf[r0 + c_off + k:r0 + c_off + k + CONV_CHUNK, :]
        c_ref[0, r0:r0 + CONV_CHUNK, :] = (cb_ref[0, r0:r0 + CONV_CHUNK, :].astype(f32) * cacc).astype(c_ref.dtype)


def _conv_branches(u, conv_a_w, conv_a_b, ln_a_g, ln_a_b, conv_c_w):
    b, s, _ = u.shape
    ts, halo = CONV_TS, CONV_HALO
    w = BRANCH_W
    ratio = ts // halo

    def main(col):
        return pl.BlockSpec((1, ts, w), lambda bi, i: (bi, i, col))

    def prev(col):
        return pl.BlockSpec((1, halo, w), lambda bi, i: (bi, jnp.maximum(i * ratio - 1, 0), col))

    def full(rows):
        return pl.BlockSpec((rows, w), lambda bi, i: (0, 0))

    cblk = OFF_C // w
    out = pl.pallas_call(
        _conv_kernel,
        grid=(b, s // ts),
        in_specs=[main(0), main(1), prev(0), prev(1),
                  main(cblk), main(cblk + 1), main(cblk + 2), prev(cblk + 1), prev(cblk + 2),
                  full(CONV_A_WIDTH * SUBLANES), full(1), full(1), full(1), full(CONV_C_WIDTH)],
        out_specs=[pl.BlockSpec((1, ts, w), lambda bi, i: (bi, i, 0)),
                   pl.BlockSpec((1, ts, w), lambda bi, i: (bi, i, 0))],
        out_shape=[jax.ShapeDtypeStruct((b, s, w), jnp.bfloat16),
                   jax.ShapeDtypeStruct((b, s, w), jnp.bfloat16)],
        scratch_shapes=[pltpu.VMEM((halo + ts, w), jnp.float32),
                        pltpu.VMEM((halo + ts, w), jnp.float32),
                        pltpu.VMEM((SUBLANES - 1, halo + ts, w), jnp.float32)],
        compiler_params=_cparams("parallel", "parallel"),
        name="conv_branches",
    )(u, u, u, u, u, u, u, u, u,
      jnp.repeat(conv_a_w, SUBLANES, axis=0), conv_a_b.reshape(1, w), ln_a_g.reshape(1, w), ln_a_b.reshape(1, w),
      conv_c_w)
    return out


def _attn_kernel(q0, q1, q2, k0, k1, k2, v0, v1, v2, o_ref, q32, k32, v32, nbuf, mbuf, zbuf, *, seq):
    f32, bf16 = jnp.float32, jnp.bfloat16
    blk = ATT_BLOCK
    scale = ATT_HEAD_DIM ** -0.5
    qi = lax.broadcasted_iota(jnp.int32, (blk, 2 * blk), 0)
    ki = lax.broadcasted_iota(jnp.int32, (blk, 2 * blk), 1)
    dist = blk + qi - ki
    valid2 = (dist >= 0) & (dist <= blk)
    valid1 = (lax.broadcasted_iota(jnp.int32, (blk, blk), 0)
              >= lax.broadcasted_iota(jnp.int32, (blk, blk), 1))

    for g, (r, qr, kr, vr) in enumerate(zip(ATT_DILATIONS, (q0, q1, q2), (k0, k1, k2), (v0, v1, v2))):
        length = seq // r
        nblk = length // blk
        if r > 1:
            q32[...] = qr[0].astype(f32)
            k32[...] = kr[0].astype(f32)
            v32[...] = vr[0].astype(f32)
        for c in range(r):
            for n in range(nblk):
                q_start = c + n * blk * r
                k_start = c + (n - 1) * blk * r if n > 0 else q_start
                k_len = 2 * blk if n > 0 else blk
                if r == 1:
                    q = qr[0, q_start:q_start + blk, :]
                    k = kr[0, k_start:k_start + k_len, :]
                    v = vr[0, k_start:k_start + k_len, :]
                else:
                    q = q32[pl.ds(q_start, blk, stride=r), :].astype(bf16)
                    k = k32[pl.ds(k_start, k_len, stride=r), :].astype(bf16)
                    v = v32[pl.ds(k_start, k_len, stride=r), :].astype(bf16)
                s = lax.dot_general(q, k, (((1,), (1,)), ((), ())), preferred_element_type=f32) * scale
                s = jnp.where(valid2 if n > 0 else valid1, s, NEG_INF)
                m = jnp.max(s, axis=-1, keepdims=True)
                p = jnp.exp(s - m)
                z = jnp.sum(p, axis=-1, keepdims=True)
                num = jnp.dot(p.astype(bf16), v, preferred_element_type=f32)
                if r == 1:
                    rows = pl.ds(q_start, blk)
                else:
                    rows = pl.ds(q_start, blk, stride=r)
                nbuf[g, rows, :] = num
                mbuf[g, rows, :] = jnp.broadcast_to(m, (blk, ATT_HEAD_DIM))
                zbuf[g, rows, :] = jnp.broadcast_to(z, (blk, ATT_HEAD_DIM))

    rows_per = 256
    for r0 in range(0, seq, rows_per):
        sl = slice(r0, r0 + rows_per)
        m_all = jnp.maximum(jnp.maximum(mbuf[0, sl, :], mbuf[1, sl, :]), mbuf[2, sl, :])
        num = jnp.zeros((rows_per, ATT_HEAD_DIM), f32)
        den = jnp.zeros((rows_per, ATT_HEAD_DIM), f32)
        for g in range(len(ATT_DILATIONS)):
            e = jnp.exp(mbuf[g, sl, :] - m_all)
            num = num + e * nbuf[g, sl, :]
            den = den + e * zbuf[g, sl, :]
        o_ref[0, sl, :] = (num / den).astype(o_ref.dtype)


def _attention(u):
    b, s, _ = u.shape
    dh = ATT_HEAD_DIM
    n_groups = len(ATT_DILATIONS)

    def spec(off, g):
        base = (off + g * BRANCH_W) // dh
        return pl.BlockSpec((1, s, dh), lambda bi, h: (bi, 0, base + h))

    in_specs = ([spec(OFF_Q, g) for g in range(n_groups)] + [spec(OFF_K, g) for g in range(n_groups)]
                + [spec(OFF_V, g) for g in range(n_groups)])
    return pl.pallas_call(
        functools.partial(_attn_kernel, seq=s),
        grid=(b, ATT_HEADS),
        in_specs=in_specs,
        out_specs=pl.BlockSpec((1, s, dh), lambda bi, h: (bi, 0, h)),
        out_shape=jax.ShapeDtypeStruct((b, s, BRANCH_W), jnp.bfloat16),
        scratch_shapes=[pltpu.VMEM((s, dh), jnp.float32)] * 3
        + [pltpu.VMEM((n_groups, s, dh), jnp.float32)] * 3,
        compiler_params=_cparams("parallel", "parallel"),
        name="dilated_attention",
    )(*([u] * 9))


def _merge_kernel(a_ref, att_ref, c_ref, g0_ref, g1_ref, g2_ref, wb_ref, o_ref):
    f32 = jnp.float32
    acc = None
    for n, (br, gr) in enumerate(((a_ref, g0_ref), (att_ref, g1_ref), (c_ref, g2_ref))):
        proj = jnp.dot(br[...], wb_ref[n], preferred_element_type=f32)
        term = _sigmoid(gr[...].astype(f32)) * proj
        acc = term if acc is None else acc + term
    o_ref[...] = acc.astype(o_ref.dtype)


def _branch_merge(a, att, c, u2d, w_branch, *, tm=512, tn=1024):
    t, w = a.shape
    d = D_MODEL
    gblk = OFF_G // tn

    def gate(n):
        return pl.BlockSpec((tm, tn), lambda j, i: (i, gblk + n * (d // tn) + j))

    row = pl.BlockSpec((tm, w), lambda j, i: (i, 0))
    return pl.pallas_call(
        _merge_kernel,
        grid=(d // tn, t // tm),
        in_specs=[row, row, row, gate(0), gate(1), gate(2),
                  pl.BlockSpec((N_BRANCHES, w, tn), lambda j, i: (0, 0, j))],
        out_specs=pl.BlockSpec((tm, tn), lambda j, i: (i, j)),
        out_shape=jax.ShapeDtypeStruct((t, d), jnp.bfloat16),
        compiler_params=_cparams("parallel", "parallel"),
        name="branch_merge",
    )(a, att, c, u2d, u2d, u2d, w_branch)


def _pack_rows(y, out_ref, rows):
    half = D_MODEL // 2
    lo = lax.bitcast_convert_type(y[:, :half].astype(jnp.bfloat16).astype(jnp.float32), jnp.int32)
    hi = lax.bitcast_convert_type(y[:, half:].astype(jnp.bfloat16).astype(jnp.float32), jnp.int32)
    word = (hi & HIGH_HALF_MASK) | lax.shift_right_logical(lo, 16)
    for c in range(PACK_SUBLANES):
        out_ref[pl.ds(c, rows, stride=PACK_SUBLANES), :] = word[:, c * LANES:(c + 1) * LANES]


def _unpack_rows(in_ref, rows):
    los, his = [], []
    for c in range(PACK_SUBLANES):
        word = in_ref[pl.ds(c, rows, stride=PACK_SUBLANES), :]
        los.append(lax.bitcast_convert_type(lax.shift_left(word, 16), jnp.float32).astype(jnp.bfloat16))
        his.append(lax.bitcast_convert_type(word & HIGH_HALF_MASK, jnp.float32).astype(jnp.bfloat16))
    return jnp.concatenate(los + his, axis=1)


def _outproj_ln_kernel(m_ref, w_ref, x_ref, g_ref, b_ref, o_ref, p_ref, *, alpha):
    y = jnp.dot(m_ref[...], w_ref[...], preferred_element_type=jnp.float32)
    x1 = _layer_norm_rows(alpha * x_ref[...] + y, g_ref[...], b_ref[...])
    o_ref[...] = x1
    _pack_rows(x1, p_ref, x1.shape[0])


def _outproj_ln(merged, w_out, x, g, b, *, alpha, tm=512):
    t, d = x.shape
    row = lambda i: (i, 0)
    const = lambda i: (0, 0)
    return pl.pallas_call(
        functools.partial(_outproj_ln_kernel, alpha=alpha),
        grid=(t // tm,),
        in_specs=[pl.BlockSpec((tm, d), row), pl.BlockSpec((d, d), const), pl.BlockSpec((tm, d), row),
                  pl.BlockSpec((1, d), const), pl.BlockSpec((1, d), const)],
        out_specs=[pl.BlockSpec((tm, d), row), pl.BlockSpec((tm * PACK_SUBLANES, LANES), row)],
        out_shape=[jax.ShapeDtypeStruct((t, d), jnp.float32),
                   jax.ShapeDtypeStruct((t * PACK_SUBLANES, LANES), jnp.int32)],
        compiler_params=_cparams("parallel"),
        name="out_proj_ln",
    )(merged, w_out, x, g.reshape(1, d), b.reshape(1, d))


def _first_argmax(vals, iota, axis, size):
    mx = jnp.max(vals, axis=axis, keepdims=True)
    idx = jnp.min(jnp.where(vals == mx, iota, size), axis=axis, keepdims=True)
    return mx, idx


def _router_kernel(x_ref, wt_ref, bias_ref, idx_ref, wts_ref, rank_ref, cnt_ref, carry):
    f32 = jnp.float32
    e, gsz, ng = MOE_EXPERTS, MOE_GROUP_SIZE, MOE_GROUPS
    tm = x_ref.shape[0]

    @pl.when(pl.program_id(0) == 0)
    def _():
        carry[...] = jnp.zeros_like(carry)

    def split(v):
        head = v.astype(jnp.bfloat16)
        return head, (v - head.astype(f32)).astype(jnp.bfloat16)

    def dot_t(a, b):
        return lax.dot_general(a, b, (((1,), (1,)), ((), ())), preferred_element_type=f32)

    w_hi, w_lo = split(wt_ref[...])
    x_hi, x_lo = split(x_ref[...])
    logits = dot_t(w_hi, x_hi) + (dot_t(w_hi, x_lo) + dot_t(w_lo, x_hi))
    scores = _sigmoid(logits)
    choice = scores + bias_ref[...]
    grp = choice.reshape(ng, gsz, tm)
    iota_in = lax.broadcasted_iota(jnp.int32, (ng, gsz, tm), 1)
    m1, i1 = _first_argmax(grp, iota_in, 1, gsz)
    m2 = jnp.max(jnp.where(iota_in == i1, -jnp.inf, grp), axis=1, keepdims=True)
    grp_score = (m1 + m2).reshape(ng, tm)
    iota_g = lax.broadcasted_iota(jnp.int32, (ng, tm), 0)
    grp_mask = jnp.zeros((ng, tm), jnp.bool_)
    for _ in range(MOE_TOPK_GROUPS):
        _, gi = _first_argmax(grp_score, iota_g, 0, ng)
        hit = iota_g == gi
        grp_mask = grp_mask | hit
        grp_score = jnp.where(hit, -jnp.inf, grp_score)
    expert_mask = jnp.broadcast_to(grp_mask.reshape(ng, 1, tm), (ng, gsz, tm)).reshape(e, tm)
    masked = jnp.where(expert_mask, choice, NEG_INF)
    iota_e = lax.broadcasted_iota(jnp.int32, (e, tm), 0)
    idxs, ws, hits = [], [], []
    for _ in range(MOE_TOPK):
        _, ei = _first_argmax(masked, iota_e, 0, e)
        hit = iota_e == ei
        idxs.append(ei)
        hits.append(hit)
        ws.append(jnp.sum(jnp.where(hit, scores, 0.0), axis=0, keepdims=True))
        masked = jnp.where(hit, -jnp.inf, masked)
    w_all = jnp.concatenate(ws, axis=0)
    idx_ref[...] = jnp.concatenate(idxs, axis=0)
    wts_ref[...] = w_all / jnp.sum(w_all, axis=0, keepdims=True) * MOE_ROUTED_SCALE

    chosen = hits[0]
    for hit in hits[1:]:
        chosen = chosen | hit
    chosen_f = jnp.where(chosen, 1.0, 0.0)
    earlier = (lax.broadcasted_iota(jnp.int32, (tm, tm), 0) < lax.broadcasted_iota(jnp.int32, (tm, tm), 1))
    prefix = jnp.dot(chosen_f.astype(jnp.bfloat16), jnp.where(earlier, 1.0, 0.0).astype(jnp.bfloat16),
                     preferred_element_type=f32)
    base = carry[:, 0:1] + prefix
    ranks = [jnp.sum(jnp.where(hit, base, 0.0), axis=0, keepdims=True) for hit in hits]
    rank_ref[...] = jnp.concatenate(ranks, axis=0).astype(jnp.int32)
    carry[...] = carry[...] + jnp.sum(chosen_f, axis=1, keepdims=True)
    cnt_ref[...] = carry[...]


def _router(x1, router_w, router_bias, *, tm=512):
    t, d = x1.shape
    e = MOE_EXPERTS
    tok = pl.BlockSpec((MOE_TOPK, tm), lambda i: (0, i))
    return pl.pallas_call(
        _router_kernel,
        grid=(t // tm,),
        in_specs=[pl.BlockSpec((tm, d), lambda i: (i, 0)),
                  pl.BlockSpec((e, d), lambda i: (0, 0)),
                  pl.BlockSpec((e, 1), lambda i: (0, 0))],
        out_specs=[tok, tok, tok, pl.BlockSpec((e, LANES), lambda i: (0, 0))],
        out_shape=[jax.ShapeDtypeStruct((MOE_TOPK, t), jnp.int32),
                   jax.ShapeDtypeStruct((MOE_TOPK, t), jnp.float32),
                   jax.ShapeDtypeStruct((MOE_TOPK, t), jnp.int32),
                   jax.ShapeDtypeStruct((e, LANES), jnp.float32)],
        scratch_shapes=[pltpu.VMEM((e, LANES), jnp.float32)],
        compiler_params=_cparams("arbitrary"),
        name="router_topk",
    )(x1, router_w.T, router_bias.reshape(e, 1).astype(jnp.float32))


def _dispatch_kernel(pad_start_ref, pad_len_ref, nvalid_ref, dest_ref, xp_ref, wsg_ref, wsu_ref, wsd_ref,
                     xs_out, sh_ref, zrows, sem, pad_sem, *, tm):
    ps = PACK_SUBLANES
    bf16 = jnp.bfloat16

    @pl.when(pl.program_id(0) == 0)
    def _():
        zrows[...] = jnp.zeros_like(zrows)
        bits = [1 << s for s in reversed(range(MOE_ROWS.bit_length() - 1))]

        def fill(e, carry, *, wait):
            off = pad_start_ref[e]
            n = pad_len_ref[e]
            for bit in bits:
                has = (n & bit) != 0

                @pl.when(has)
                def _(off=off, bit=bit):
                    cp = pltpu.make_async_copy(zrows.at[pl.ds(0, bit * ps), :],
                                               xs_out.at[pl.ds(pl.multiple_of(off * ps, ps), bit * ps), :], pad_sem)
                    if wait:
                        cp.wait()
                    else:
                        cp.start()

                off = off + jnp.where(has, bit, 0)
            return carry

        half_rows = MOE_ROWS // 2

        def fill_tail(h, carry, *, wait):
            cp = pltpu.make_async_copy(zrows, xs_out.at[pl.ds(pl.multiple_of(h * half_rows * ps, ps), half_rows * ps), :],
                                       pad_sem)
            if wait:
                cp.wait()
            else:
                cp.start()
            return carry

        tail = (2 * nvalid_ref[0], 2 * (xs_out.shape[0] // (MOE_ROWS * ps)))
        lax.fori_loop(0, MOE_EXPERTS, functools.partial(fill, wait=False), 0)
        lax.fori_loop(*tail, functools.partial(fill_tail, wait=False), 0)
        lax.fori_loop(0, MOE_EXPERTS, functools.partial(fill, wait=True), 0)
        lax.fori_loop(*tail, functools.partial(fill_tail, wait=True), 0)

    def start_rows(k):
        for tok in range(tm):
            row = dest_ref[0, 0, k * tm + tok]
            pltpu.make_async_copy(xp_ref.at[pl.ds(tok * ps, ps), :],
                                  xs_out.at[pl.ds(pl.multiple_of(row * ps, ps), ps), :],
                                  sem).start(priority=tok % DMA_PRIORITIES)

    xb = _unpack_rows(xp_ref, tm)
    for k in range(MOE_TOPK // 2):
        start_rows(k)
    gate = jnp.dot(xb, wsg_ref[...].astype(bf16), preferred_element_type=jnp.float32)
    up = jnp.dot(xb, wsu_ref[...].astype(bf16), preferred_element_type=jnp.float32)
    act = _silu(gate) * up
    for k in range(MOE_TOPK // 2, MOE_TOPK):
        start_rows(k)
    sh_ref[...] = jnp.dot(act.astype(bf16), wsd_ref[...].astype(bf16), preferred_element_type=jnp.float32)
    for _ in range(MOE_TOPK):
        pltpu.make_async_copy(xp_ref, xs_out.at[pl.ds(0, tm * ps), :], sem).wait()


def _dispatch_shared(x1p, dest_kt, pad_start, pad_len, nvalid, n_rows, ws_gate, ws_up, ws_down, layer, *, tm=512):
    k, t = dest_kt.shape
    ps, d = PACK_SUBLANES, D_MODEL
    nsteps = t // tm
    dest3 = dest_kt.reshape(k, nsteps, tm).transpose(1, 0, 2).reshape(nsteps, 1, k * tm)
    wspec = lambda shape: pl.BlockSpec((None,) + shape, lambda i, *_: (layer, 0, 0))
    grid_spec = pltpu.PrefetchScalarGridSpec(
        num_scalar_prefetch=3,
        grid=(nsteps,),
        in_specs=[pl.BlockSpec((1, 1, k * tm), lambda i, *_: (i, 0, 0), memory_space=pltpu.SMEM),
                  pl.BlockSpec((tm * ps, LANES), lambda i, *_: (i, 0)),
                  wspec((d, MOE_FFN)), wspec((d, MOE_FFN)), wspec((MOE_FFN, d))],
        out_specs=[pl.BlockSpec(memory_space=pl.ANY), pl.BlockSpec((tm, d), lambda i, *_: (i, 0))],
        scratch_shapes=[pltpu.VMEM((MOE_ROWS // 2 * ps, LANES), jnp.int32),
                        pltpu.SemaphoreType.DMA(()), pltpu.SemaphoreType.DMA(())],
    )
    return pl.pallas_call(
        functools.partial(_dispatch_kernel, tm=tm),
        grid_spec=grid_spec,
        out_shape=[jax.ShapeDtypeStruct((n_rows * ps, LANES), jnp.int32),
                   jax.ShapeDtypeStruct((t, d), jnp.float32)],
        compiler_params=_cparams("arbitrary"),
        name="dispatch_shared",
    )(pad_start, pad_len, nvalid, dest3, x1p, ws_gate, ws_up, ws_down)


def _expert_kernel(blk_e_ref, nvalid_ref, xs_ref, wg_ref, wu_ref, wd_ref, y_ref, wg_bf, wu_bf, wd_bf):
    i = pl.program_id(0)
    rows = MOE_ROWS
    bf16 = jnp.bfloat16

    @pl.when((i == 0) | (blk_e_ref[i] != blk_e_ref[jnp.maximum(i - 1, 0)]))
    def _():
        wg_bf[...] = wg_ref[...].astype(bf16)
        wu_bf[...] = wu_ref[...].astype(bf16)
        wd_bf[...] = wd_ref[...].astype(bf16)

    @pl.when(i < nvalid_ref[0])
    def _():
        xb = _unpack_rows(xs_ref, rows)
        gate = jnp.dot(xb, wg_bf[...], preferred_element_type=jnp.float32)
        up = jnp.dot(xb, wu_bf[...], preferred_element_type=jnp.float32)
        act = _silu(gate) * up
        y = jnp.dot(act.astype(bf16), wd_bf[...], preferred_element_type=jnp.float32)
        _pack_rows(y, y_ref, rows)

    @pl.when(i >= nvalid_ref[0])
    def _():
        y_ref[...] = jnp.zeros_like(y_ref)


def _routed_experts(xs, blk_expert, nvalid, w_gate, w_up, w_down, layer):
    nb = blk_expert.shape[0]
    rows, d = MOE_ROWS, D_MODEL
    wspec = lambda shape: pl.BlockSpec((None, None) + shape, lambda i, be, nv: (layer, be[i], 0, 0))
    grid_spec = pltpu.PrefetchScalarGridSpec(
        num_scalar_prefetch=2,
        grid=(nb,),
        in_specs=[pl.BlockSpec((rows * PACK_SUBLANES, LANES), lambda i, be, nv: (jnp.minimum(i, nv[0] - 1), 0)),
                  wspec((d, MOE_FFN)), wspec((d, MOE_FFN)), wspec((MOE_FFN, d))],
        out_specs=pl.BlockSpec((rows * PACK_SUBLANES, LANES), lambda i, be, nv: (i, 0)),
        scratch_shapes=[pltpu.VMEM((d, MOE_FFN), jnp.bfloat16), pltpu.VMEM((d, MOE_FFN), jnp.bfloat16),
                        pltpu.VMEM((MOE_FFN, d), jnp.bfloat16)],
    )
    return pl.pallas_call(
        _expert_kernel,
        grid_spec=grid_spec,
        out_shape=jax.ShapeDtypeStruct((nb * rows * PACK_SUBLANES, LANES), jnp.int32),
        compiler_params=_cparams("arbitrary"),
        name="routed_experts",
    )(blk_expert, nvalid, xs, w_gate, w_up, w_down)


def _combine_kernel(dest_ref, dest_next_ref, ys_hbm, w_ref, x_ref, sh_ref, g_ref, b_ref, o_ref, ob_ref,
                    ybuf0, ybuf1, acc, sem, *, alpha, nsteps):
    i = pl.program_id(0)
    tc, ps = COMBINE_TOK, PACK_SUBLANES
    n_rows = tc * MOE_TOPK
    half = D_MODEL // 2
    f32 = jnp.float32
    bufs = (ybuf0, ybuf1)

    def row_copy(d_ref, r, slot):
        row = d_ref[0, 0, r]
        return pltpu.make_async_copy(ys_hbm.at[pl.ds(pl.multiple_of(row * ps, ps), ps), :],
                                     bufs[slot].at[pl.ds(pl.multiple_of(r * ps, ps), ps), :], sem.at[slot])

    def wait_block(slot):
        pltpu.make_async_copy(ys_hbm.at[pl.ds(0, n_rows * ps), :], bufs[slot], sem.at[slot]).wait()

    @pl.when(i == 0)
    def _():
        def body(pair, carry):
            for prio in range(DMA_PRIORITIES):
                row_copy(dest_ref, pair * DMA_PRIORITIES + prio, 0).start(priority=prio)
            return carry
        lax.fori_loop(0, n_rows // DMA_PRIORITIES, body, 0, unroll=4)

    rows_per_chunk = n_rows // ps
    for slot in range(2):
        @pl.when(lax.rem(i, 2) == slot)
        def _(slot=slot):
            wait_block(slot)
            for c in range(ps):
                for r in range(c * rows_per_chunk, (c + 1) * rows_per_chunk):
                    row_copy(dest_next_ref, r, 1 - slot).start(priority=r % DMA_PRIORITIES)
                lo_cols = slice(c * LANES, (c + 1) * LANES)
                hi_cols = slice(half + c * LANES, half + (c + 1) * LANES)
                a_lo = alpha * x_ref[:, lo_cols] + sh_ref[:, lo_cols]
                a_hi = alpha * x_ref[:, hi_cols] + sh_ref[:, hi_cols]
                for k in range(MOE_TOPK):
                    word = bufs[slot][pl.ds(k * tc * ps + c, tc, stride=ps), :]
                    wk = w_ref[:, k:k + 1]
                    a_lo = a_lo + wk * lax.bitcast_convert_type(lax.shift_left(word, 16), f32)
                    a_hi = a_hi + wk * lax.bitcast_convert_type(word & HIGH_HALF_MASK, f32)
                acc[:, lo_cols] = a_lo
                acc[:, hi_cols] = a_hi

            @pl.when(i == nsteps - 1)
            def _():
                wait_block(1 - slot)

    x2 = _layer_norm_rows(acc[...], g_ref[...], b_ref[...])
    o_ref[...] = x2
    ob_ref[...] = x2.astype(ob_ref.dtype)


def _combine_ln(ys, dest_kt, wts_tk, x1, shared, g, b, *, alpha):
    t, d = x1.shape
    tc = COMBINE_TOK
    nsteps = t // tc
    dest3 = dest_kt.reshape(MOE_TOPK, nsteps, tc).transpose(1, 0, 2).reshape(nsteps, 1, MOE_TOPK * tc)
    smem_blk = lambda f: pl.BlockSpec((1, 1, MOE_TOPK * tc), f, memory_space=pltpu.SMEM)
    row = lambda i: (i, 0)
    const = lambda i: (0, 0)
    return pl.pallas_call(
        functools.partial(_combine_kernel, alpha=alpha, nsteps=nsteps),
        grid=(nsteps,),
        in_specs=[smem_blk(lambda i: (i, 0, 0)),
                  smem_blk(lambda i: (jnp.minimum(i + 1, nsteps - 1), 0, 0)),
                  pl.BlockSpec(memory_space=pl.ANY),
                  pl.BlockSpec((tc, MOE_TOPK), row),
                  pl.BlockSpec((tc, d), row), pl.BlockSpec((tc, d), row),
                  pl.BlockSpec((1, d), const), pl.BlockSpec((1, d), const)],
        out_specs=[pl.BlockSpec((tc, d), row), pl.BlockSpec((tc, d), row)],
        out_shape=[jax.ShapeDtypeStruct((t, d), jnp.float32), jax.ShapeDtypeStruct((t, d), jnp.bfloat16)],
        scratch_shapes=[pltpu.VMEM((MOE_TOPK * tc * PACK_SUBLANES, LANES), jnp.int32),
                        pltpu.VMEM((MOE_TOPK * tc * PACK_SUBLANES, LANES), jnp.int32),
                        pltpu.VMEM((tc, d), jnp.float32),
                        pltpu.SemaphoreType.DMA((2,))],
        compiler_params=_cparams("arbitrary"),
        name="combine_ln",
    )(dest3, dest3, ys, wts_tk, x1, shared, g.reshape(1, d), b.reshape(1, d))


def _dispatch_plan(idx_kt, rank_kt, counts):
    k, t = idx_kt.shape
    e, rows = MOE_EXPERTS, MOE_ROWS
    nb = k * t // rows + e
    padded = (counts + rows - 1) // rows * rows
    pend = jnp.cumsum(padded)
    pstart = pend - padded
    blk_start = jnp.arange(nb, dtype=jnp.int32) * rows
    blk_expert = jnp.minimum(jnp.sum(pend[None, :] <= blk_start[:, None], axis=1), e - 1).astype(jnp.int32)
    nvalid = (pend[-1] // rows).astype(jnp.int32).reshape(1)
    onehot = idx_kt[:, :, None] == jnp.arange(e, dtype=jnp.int32)
    dest = rank_kt + jnp.sum(jnp.where(onehot, pstart, 0), axis=-1).astype(jnp.int32)
    pad_start = (pstart + counts).astype(jnp.int32)
    pad_len = (padded - counts).astype(jnp.int32)
    return blk_expert, nvalid, dest, pad_start, pad_len, nb


def kernel(x, w_in, conv_a_w, conv_a_b, ln_a_g, ln_a_b, conv_c_w, w_branch, w_out, ln1_g, ln1_b, router_w,
           router_bias, w_gate, w_up, w_down, ws_gate, ws_up, ws_down, ln2_g, ln2_b):
    b, s, d = x.shape
    depth = w_in.shape[0]
    t = b * s
    alpha = (DEPTH_ALPHA_BASE * depth) ** 0.25
    bf16 = jnp.bfloat16
    xf = x.reshape(t, d)
    xb = xf.astype(bf16)
    for l in range(depth):
        u = _in_proj(xb, w_in, l, tm=min(t, 1024), tn=1024)
        u3 = u.reshape(b, s, N_IN)
        a, c = _conv_branches(u3, conv_a_w[l], conv_a_b[l], ln_a_g[l], ln_a_b[l], conv_c_w[l])
        att = _attention(u3)
        merged = _branch_merge(a.reshape(t, BRANCH_W), att.reshape(t, BRANCH_W), c.reshape(t, BRANCH_W), u,
                               w_branch[l].astype(bf16))
        x1, x1p = _outproj_ln(merged, w_out[l].astype(bf16), xf, ln1_g[l], ln1_b[l], alpha=alpha)
        idx_kt, wts_kt, rank_kt, counts = _router(x1, router_w[l], router_bias[l])
        blk_expert, nvalid, dest, pad_start, pad_len, nb = _dispatch_plan(idx_kt, rank_kt,
                                                                          counts[:, 0].astype(jnp.int32))
        xs, shared = _dispatch_shared(x1p, dest, pad_start, pad_len, nvalid, nb * MOE_ROWS,
                                      ws_gate, ws_up, ws_down, l)
        ys = _routed_experts(xs, blk_expert, nvalid, w_gate, w_up, w_down, l)
        xf, xb = _combine_ln(ys, dest, wts_kt.T, x1, shared, ln2_g[l], ln2_b[l], alpha=alpha)
    return xf.reshape(b, s, d)
```

```python
import functools

import jax
import jax.numpy as jnp
from jax import lax
from jax.experimental import pallas as pl
from jax.experimental.pallas import tpu as pltpu

D_MODEL = 2048
BRANCH_W = D_MODEL // 2
N_BRANCHES = 3
CONV_A_WIDTH = 31
CONV_C_WIDTH = 3
ATT_DILATIONS = (1, 4, 16)
ATT_BLOCK = 128
ATT_HEADS = 8
ATT_HEAD_DIM = BRANCH_W // ATT_HEADS
ATT_W = len(ATT_DILATIONS) * BRANCH_W
OFF_Q = 2 * BRANCH_W
OFF_K = OFF_Q + ATT_W
OFF_V = OFF_K + ATT_W
OFF_C = OFF_V + ATT_W
OFF_G = OFF_C + 3 * BRANCH_W
N_IN = OFF_G + N_BRANCHES * D_MODEL
MOE_EXPERTS = 64
MOE_TOPK = 8
MOE_GROUPS = 8
MOE_GROUP_SIZE = MOE_EXPERTS // MOE_GROUPS
MOE_TOPK_GROUPS = 4
MOE_FFN = D_MODEL // 8
MOE_ROUTED_SCALE = 2.5
LN_EPS = 1e-5
NEG_INF = -1e30
DEPTH_ALPHA_BASE = 2.0

LANES = 128
VMEM_LIMIT_BYTES = 52 * 1024 * 1024
MOE_ROWS = 512
PACK_SUBLANES = D_MODEL // 2 // LANES
SUBLANES = 8
HIGH_HALF_MASK = -65536
CONV_TS = 512
CONV_HALO = 32
CONV_CHUNK = 32
COMBINE_TOK = 128
DMA_PRIORITIES = 2


def _cparams(*sem):
    return pltpu.CompilerParams(dimension_semantics=sem, vmem_limit_bytes=VMEM_LIMIT_BYTES)


def _sigmoid(v):
    return 1.0 / (1.0 + jnp.exp(-v))


def _silu(v):
    return v * _sigmoid(v)


def _layer_norm_rows(v, g, b):
    mu = jnp.mean(v, axis=-1, keepdims=True)
    vc = v - mu
    var = jnp.mean(vc * vc, axis=-1, keepdims=True)
    return vc * lax.rsqrt(var + LN_EPS) * g + b


def _in_proj_kernel(x_ref, w_ref, o_ref, w_bf):
    @pl.when(pl.program_id(1) == 0)
    def _():
        w_bf[...] = w_ref[...].astype(jnp.bfloat16)

    o_ref[...] = jnp.dot(x_ref[...], w_bf[...], preferred_element_type=jnp.float32).astype(o_ref.dtype)


def _in_proj(x, w_in, layer, *, tm, tn):
    m, k = x.shape
    n = w_in.shape[-1]
    return pl.pallas_call(
        _in_proj_kernel,
        grid=(n // tn, m // tm),
        in_specs=[pl.BlockSpec((tm, k), lambda j, i: (i, 0)),
                  pl.BlockSpec((None, k, tn), lambda j, i: (layer, 0, j))],
        out_specs=pl.BlockSpec((tm, tn), lambda j, i: (i, j)),
        out_shape=jax.ShapeDtypeStruct((m, n), jnp.bfloat16),
        scratch_shapes=[pltpu.VMEM((k, tn), jnp.bfloat16)],
        compiler_params=_cparams("parallel", "arbitrary"),
        name="in_proj",
    )(x, w_in)


def _conv_kernel(val_ref, gate_ref, hval_ref, hgate_ref, cb_ref, cc_ref, ch_ref, hcc_ref, hch_ref,
                 wa_ref, ba_ref, ga_ref, bta_ref, wc_ref, a_ref, c_ref, abuf, cbuf, shifted):
    first = pl.program_id(1) == 0
    f32 = jnp.float32
    glu_h = hval_ref[0].astype(f32) * _sigmoid(hgate_ref[0].astype(f32))
    abuf[0:CONV_HALO, :] = jnp.where(first, 0.0, glu_h)
    abuf[CONV_HALO:, :] = val_ref[0].astype(f32) * _sigmoid(gate_ref[0].astype(f32))
    prod_h = hcc_ref[0].astype(f32) * hch_ref[0].astype(f32)
    cbuf[0:CONV_HALO, :] = jnp.where(first, 0.0, prod_h)
    cbuf[CONV_HALO:, :] = cc_ref[0].astype(f32) * ch_ref[0].astype(f32)

    span = CONV_HALO + CONV_TS - SUBLANES
    for j in range(1, SUBLANES):
        shifted[j - 1, 0:span, :] = abuf[j:j + span, :]

    a_off = CONV_HALO - (CONV_A_WIDTH - 1)
    c_off = CONV_HALO - (CONV_C_WIDTH - 1)
    groups = CONV_CHUNK // SUBLANES
    for r0 in range(0, CONV_TS, CONV_CHUNK):
        acc = jnp.zeros((groups, SUBLANES, BRANCH_W), f32)
        for k in range(CONV_A_WIDTH):
            j = (a_off + k) % SUBLANES
            base = r0 + a_off + k - j
            rows = abuf[base:base + CONV_CHUNK, :] if j == 0 else shifted[j - 1, base:base + CONV_CHUNK, :]
            tap = wa_ref[k * SUBLANES:(k + 1) * SUBLANES, :]
            acc = acc + tap[None] * rows.reshape(groups, SUBLANES, BRANCH_W)
        acc = acc.reshape(CONV_CHUNK, BRANCH_W)
        y = _layer_norm_rows(acc + ba_ref[...], ga_ref[...], bta_ref[...])
        a_ref[0, r0:r0 + CONV_CHUNK, :] = _silu(y).astype(a_ref.dtype)
        cacc = jnp.zeros((CONV_CHUNK, BRANCH_W), f32)
        for k in range(CONV_C_WIDTH):
            cacc = cacc + wc_ref[k:k + 1, :] * cbuf[r0 + c_off + k:r0 + c_off + k + CONV_CHUNK, :]
        c_ref[0, r0:r0 + CONV_CHUNK, :] = (cb_ref[0, r0:r0 + CONV_CHUNK, :].astype(f32) * cacc).astype(c_ref.dtype)


def _conv_branches(u, conv_a_w, conv_a_b, ln_a_g, ln_a_b, conv_c_w):
    b, s, _ = u.shape
    ts, halo = CONV_TS, CONV_HALO
    w = BRANCH_W
    ratio = ts // halo

    def main(col):
        return pl.BlockSpec((1, ts, w), lambda bi, i: (bi, i, col))

    def prev(col):
        return pl.BlockSpec((1, halo, w), lambda bi, i: (bi, jnp.maximum(i * ratio - 1, 0), col))

    def full(rows):
        return pl.BlockSpec((rows, w), lambda bi, i: (0, 0))

    cblk = OFF_C // w
    out = pl.pallas_call(
        _conv_kernel,
        grid=(b, s // ts),
        in_specs=[main(0), main(1), prev(0), prev(1),
                  main(cblk), main(cblk + 1), main(cblk + 2), prev(cblk + 1), prev(cblk + 2),
                  full(CONV_A_WIDTH * SUBLANES), full(1), full(1), full(1), full(CONV_C_WIDTH)],
        out_specs=[pl.BlockSpec((1, ts, w), lambda bi, i: (bi, i, 0)),
                   pl.BlockSpec((1, ts, w), lambda bi, i: (bi, i, 0))],
        out_shape=[jax.ShapeDtypeStruct((b, s, w), jnp.bfloat16),
                   jax.ShapeDtypeStruct((b, s, w), jnp.bfloat16)],
        scratch_shapes=[pltpu.VMEM((halo + ts, w), jnp.float32),
                        pltpu.VMEM((halo + ts, w), jnp.float32),
                        pltpu.VMEM((SUBLANES - 1, halo + ts, w), jnp.float32)],
        compiler_params=_cparams("parallel", "parallel"),
        name="conv_branches",
    )(u, u, u, u, u, u, u, u, u,
      jnp.repeat(conv_a_w, SUBLANES, axis=0), conv_a_b.reshape(1, w), ln_a_g.reshape(1, w), ln_a_b.reshape(1, w),
      conv_c_w)
    return out


def _attn_kernel(q0, q1, q2, k0, k1, k2, v0, v1, v2, o_ref, q32, k32, v32, nbuf, mbuf, zbuf, *, seq):
    f32, bf16 = jnp.float32, jnp.bfloat16
    blk = ATT_BLOCK
    scale = ATT_HEAD_DIM ** -0.5
    qi = lax.broadcasted_iota(jnp.int32, (blk, 2 * blk), 0)
    ki = lax.broadcasted_iota(jnp.int32, (blk, 2 * blk), 1)
    dist = blk + qi - ki
    valid2 = (dist >= 0) & (dist <= blk)
    valid1 = (lax.broadcasted_iota(jnp.int32, (blk, blk), 0)
              >= lax.broadcasted_iota(jnp.int32, (blk, blk), 1))

    for g, (r, qr, kr, vr) in enumerate(zip(ATT_DILATIONS, (q0, q1, q2), (k0, k1, k2), (v0, v1, v2))):
        length = seq // r
        nblk = length // blk
        if r > 1:
            q32[...] = qr[0].astype(f32)
            k32[...] = kr[0].astype(f32)
            v32[...] = vr[0].astype(f32)
        for c in range(r):
            for n in range(nblk):
                q_start = c + n * blk * r
                k_start = c + (n - 1) * blk * r if n > 0 else q_start
                k_len = 2 * blk if n > 0 else blk
                if r == 1:
                    q = qr[0, q_start:q_start + blk, :]
                    k = kr[0, k_start:k_start + k_len, :]
                    v = vr[0, k_start:k_start + k_len, :]
                else:
                    q = q32[pl.ds(q_start, blk, stride=r), :].astype(bf16)
                    k = k32[pl.ds(k_start, k_len, stride=r), :].astype(bf16)
                    v = v32[pl.ds(k_start, k_len, stride=r), :].astype(bf16)
                s = lax.dot_general(q, k, (((1,), (1,)), ((), ())), preferred_element_type=f32) * scale
                s = jnp.where(valid2 if n > 0 else valid1, s, NEG_INF)
                m = jnp.max(s, axis=-1, keepdims=True)
                p = jnp.exp(s - m)
                z = jnp.sum(p, axis=-1, keepdims=True)
                num = jnp.dot(p.astype(bf16), v, preferred_element_type=f32)
                if r == 1:
                    rows = pl.ds(q_start, blk)
                else:
                    rows = pl.ds(q_start, blk, stride=r)
                nbuf[g, rows, :] = num
                mbuf[g, rows, :] = jnp.broadcast_to(m, (blk, ATT_HEAD_DIM))
                zbuf[g, rows, :] = jnp.broadcast_to(z, (blk, ATT_HEAD_DIM))

    rows_per = 256
    for r0 in range(0, seq, rows_per):
        sl = slice(r0, r0 + rows_per)
        m_all = jnp.maximum(jnp.maximum(mbuf[0, sl, :], mbuf[1, sl, :]), mbuf[2, sl, :])
        num = jnp.zeros((rows_per, ATT_HEAD_DIM), f32)
        den = jnp.zeros((rows_per, ATT_HEAD_DIM), f32)
        for g in range(len(ATT_DILATIONS)):
            e = jnp.exp(mbuf[g, sl, :] - m_all)
            num = num + e * nbuf[g, sl, :]
            den = den + e * zbuf[g, sl, :]
        o_ref[0, sl, :] = (num / den).astype(o_ref.dtype)


def _attention(u):
    b, s, _ = u.shape
    dh = ATT_HEAD_DIM
    n_groups = len(ATT_DILATIONS)

    def spec(off, g):
        base = (off + g * BRANCH_W) // dh
        return pl.BlockSpec((1, s, dh), lambda bi, h: (bi, 0, base + h))

    in_specs = ([spec(OFF_Q, g) for g in range(n_groups)] + [spec(OFF_K, g) for g in range(n_groups)]
                + [spec(OFF_V, g) for g in range(n_groups)])
    return pl.pallas_call(
        functools.partial(_attn_kernel, seq=s),
        grid=(b, ATT_HEADS),
        in_specs=in_specs,
        out_specs=pl.BlockSpec((1, s, dh), lambda bi, h: (bi, 0, h)),
        out_shape=jax.ShapeDtypeStruct((b, s, BRANCH_W), jnp.bfloat16),
        scratch_shapes=[pltpu.VMEM((s, dh), jnp.float32)] * 3
        + [pltpu.VMEM((n_groups, s, dh), jnp.float32)] * 3,
        compiler_params=_cparams("parallel", "parallel"),
        name="dilated_attention",
    )(*([u] * 9))


def _merge_kernel(a_ref, att_ref, c_ref, g0_ref, g1_ref, g2_ref, wb_ref, o_ref):
    f32 = jnp.float32
    acc = None
    for n, (br, gr) in enumerate(((a_ref, g0_ref), (att_ref, g1_ref), (c_ref, g2_ref))):
        proj = jnp.dot(br[...], wb_ref[n], preferred_element_type=f32)
        term = _sigmoid(gr[...].astype(f32)) * proj
        acc = term if acc is None else acc + term
    o_ref[...] = acc.astype(o_ref.dtype)


def _branch_merge(a, att, c, u2d, w_branch, *, tm=1024, tn=1024):
    t, w = a.shape
    d = D_MODEL
    gblk = OFF_G // tn

    def gate(n):
        return pl.BlockSpec((tm, tn), lambda j, i: (i, gblk + n * (d // tn) + j))

    row = pl.BlockSpec((tm, w), lambda j, i: (i, 0))
    return pl.pallas_call(
        _merge_kernel,
        grid=(d // tn, t // tm),
        in_specs=[row, row, row, gate(0), gate(1), gate(2),
                  pl.BlockSpec((N_BRANCHES, w, tn), lambda j, i: (0, 0, j))],
        out_specs=pl.BlockSpec((tm, tn), lambda j, i: (i, j)),
        out_shape=jax.ShapeDtypeStruct((t, d), jnp.bfloat16),
        compiler_params=_cparams("parallel", "parallel"),
        name="branch_merge",
    )(a, att, c, u2d, u2d, u2d, w_branch)


def _pack_rows(y, out_ref, rows):
    half = D_MODEL // 2
    lo = lax.bitcast_convert_type(y[:, :half].astype(jnp.bfloat16).astype(jnp.float32), jnp.int32)
    hi = lax.bitcast_convert_type(y[:, half:].astype(jnp.bfloat16).astype(jnp.float32), jnp.int32)
    word = (hi & HIGH_HALF_MASK) | lax.shift_right_logical(lo, 16)
    for c in range(PACK_SUBLANES):
        out_ref[pl.ds(c, rows, stride=PACK_SUBLANES), :] = word[:, c * LANES:(c + 1) * LANES]


def _unpack_rows(in_ref, rows):
    los, his = [], []
    for c in range(PACK_SUBLANES):
        word = in_ref[pl.ds(c, rows, stride=PACK_SUBLANES), :]
        los.append(lax.bitcast_convert_type(lax.shift_left(word, 16), jnp.float32).astype(jnp.bfloat16))
        his.append(lax.bitcast_convert_type(word & HIGH_HALF_MASK, jnp.float32).astype(jnp.bfloat16))
    return jnp.concatenate(los + his, axis=1)


def _outproj_ln_kernel(m_ref, w_ref, x_ref, g_ref, b_ref, o_ref, p_ref, *, alpha):
    half = m_ref.shape[0] // 2
    for h in range(2):
        rows = slice(h * half, (h + 1) * half)
        y = jnp.dot(m_ref[rows, :], w_ref[...], preferred_element_type=jnp.float32)
        x1 = _layer_norm_rows(alpha * x_ref[rows, :] + y, g_ref[...], b_ref[...])
        o_ref[rows, :] = x1
        _pack_rows(x1, p_ref.at[pl.ds(h * half * PACK_SUBLANES, half * PACK_SUBLANES), :], half)


def _outproj_ln(merged, w_out, x, g, b, *, alpha, tm=512):
    t, d = x.shape
    row = lambda i: (i, 0)
    const = lambda i: (0, 0)
    return pl.pallas_call(
        functools.partial(_outproj_ln_kernel, alpha=alpha),
        grid=(t // tm,),
        in_specs=[pl.BlockSpec((tm, d), row), pl.BlockSpec((d, d), const), pl.BlockSpec((tm, d), row),
                  pl.BlockSpec((1, d), const), pl.BlockSpec((1, d), const)],
        out_specs=[pl.BlockSpec((tm, d), row), pl.BlockSpec((tm * PACK_SUBLANES, LANES), row)],
        out_shape=[jax.ShapeDtypeStruct((t, d), jnp.float32),
                   jax.ShapeDtypeStruct((t * PACK_SUBLANES, LANES), jnp.int32)],
        compiler_params=_cparams("parallel"),
        name="out_proj_ln",
    )(merged, w_out, x, g.reshape(1, d), b.reshape(1, d))


def _first_argmax(vals, iota, axis, size):
    mx = jnp.max(vals, axis=axis, keepdims=True)
    idx = jnp.min(jnp.where(vals == mx, iota, size), axis=axis, keepdims=True)
    return mx, idx


def _router_kernel(x_ref, wt_ref, bias_ref, idx_ref, wts_ref, rank_ref, cnt_ref, carry):
    f32 = jnp.float32
    e, gsz, ng = MOE_EXPERTS, MOE_GROUP_SIZE, MOE_GROUPS
    tm = x_ref.shape[0]

    @pl.when(pl.program_id(0) == 0)
    def _():
        carry[...] = jnp.zeros_like(carry)

    def split(v):
        head = v.astype(jnp.bfloat16)
        return head, (v - head.astype(f32)).astype(jnp.bfloat16)

    def dot_t(a, b):
        return lax.dot_general(a, b, (((1,), (1,)), ((), ())), preferred_element_type=f32)

    w_hi, w_lo = split(wt_ref[...])
    x_hi, x_lo = split(x_ref[...])
    logits = dot_t(w_hi, x_hi) + (dot_t(w_hi, x_lo) + dot_t(w_lo, x_hi))
    scores = _sigmoid(logits)
    choice = scores + bias_ref[...]
    grp = choice.reshape(ng, gsz, tm)
    iota_in = lax.broadcasted_iota(jnp.int32, (ng, gsz, tm), 1)
    m1, i1 = _first_argmax(grp, iota_in, 1, gsz)
    m2 = jnp.max(jnp.where(iota_in == i1, -jnp.inf, grp), axis=1, keepdims=True)
    grp_score = (m1 + m2).reshape(ng, tm)
    iota_g = lax.broadcasted_iota(jnp.int32, (ng, tm), 0)
    grp_mask = jnp.zeros((ng, tm), jnp.bool_)
    for _ in range(MOE_TOPK_GROUPS):
        _, gi = _first_argmax(grp_score, iota_g, 0, ng)
        hit = iota_g == gi
        grp_mask = grp_mask | hit
        grp_score = jnp.where(hit, -jnp.inf, grp_score)
    expert_mask = jnp.broadcast_to(grp_mask.reshape(ng, 1, tm), (ng, gsz, tm)).reshape(e, tm)
    masked = jnp.where(expert_mask, choice, NEG_INF)
    iota_e = lax.broadcasted_iota(jnp.int32, (e, tm), 0)
    idxs, ws, hits = [], [], []
    for _ in range(MOE_TOPK):
        _, ei = _first_argmax(masked, iota_e, 0, e)
        hit = iota_e == ei
        idxs.append(ei)
        hits.append(hit)
        ws.append(jnp.sum(jnp.where(hit, scores, 0.0), axis=0, keepdims=True))
        masked = jnp.where(hit, -jnp.inf, masked)
    w_all = jnp.concatenate(ws, axis=0)
    idx_ref[...] = jnp.concatenate(idxs, axis=0)
    wts_ref[...] = w_all / jnp.sum(w_all, axis=0, keepdims=True) * MOE_ROUTED_SCALE

    chosen = hits[0]
    for hit in hits[1:]:
        chosen = chosen | hit
    chosen_f = jnp.where(chosen, 1.0, 0.0)
    earlier = (lax.broadcasted_iota(jnp.int32, (tm, tm), 0) < lax.broadcasted_iota(jnp.int32, (tm, tm), 1))
    prefix = jnp.dot(chosen_f.astype(jnp.bfloat16), jnp.where(earlier, 1.0, 0.0).astype(jnp.bfloat16),
                     preferred_element_type=f32)
    base = carry[:, 0:1] + prefix
    ranks = [jnp.sum(jnp.where(hit, base, 0.0), axis=0, keepdims=True) for hit in hits]
    rank_ref[...] = jnp.concatenate(ranks, axis=0).astype(jnp.int32)
    carry[...] = carry[...] + jnp.sum(chosen_f, axis=1, keepdims=True)
    cnt_ref[...] = carry[...]


def _router(x1, router_w, router_bias, *, tm=512):
    t, d = x1.shape
    e = MOE_EXPERTS
    tok = pl.BlockSpec((MOE_TOPK, tm), lambda i: (0, i))
    return pl.pallas_call(
        _router_kernel,
        grid=(t // tm,),
        in_specs=[pl.BlockSpec((tm, d), lambda i: (i, 0)),
                  pl.BlockSpec((e, d), lambda i: (0, 0)),
                  pl.BlockSpec((e, 1), lambda i: (0, 0))],
        out_specs=[tok, tok, tok, pl.BlockSpec((e, LANES), lambda i: (0, 0))],
        out_shape=[jax.ShapeDtypeStruct((MOE_TOPK, t), jnp.int32),
                   jax.ShapeDtypeStruct((MOE_TOPK, t), jnp.float32),
                   jax.ShapeDtypeStruct((MOE_TOPK, t), jnp.int32),
                   jax.ShapeDtypeStruct((e, LANES), jnp.float32)],
        scratch_shapes=[pltpu.VMEM((e, LANES), jnp.float32)],
        compiler_params=_cparams("arbitrary"),
        name="router_topk",
    )(x1, router_w.T, router_bias.reshape(e, 1).astype(jnp.float32))


def _dispatch_kernel(pad_start_ref, pad_len_ref, nvalid_ref, dest_ref, xp_ref, wsg_ref, wsu_ref, wsd_ref,
                     xs_out, sh_ref, zrows, sem, pad_sem, *, tm):
    ps = PACK_SUBLANES
    bf16 = jnp.bfloat16

    @pl.when(pl.program_id(0) == 0)
    def _():
        zrows[...] = jnp.zeros_like(zrows)
        bits = [1 << s for s in reversed(range(MOE_ROWS.bit_length() - 1))]

        def fill(e, carry, *, wait):
            off = pad_start_ref[e]
            n = pad_len_ref[e]
            for bit in bits:
                has = (n & bit) != 0

                @pl.when(has)
                def _(off=off, bit=bit):
                    cp = pltpu.make_async_copy(zrows.at[pl.ds(0, bit * ps), :],
                                               xs_out.at[pl.ds(pl.multiple_of(off * ps, ps), bit * ps), :], pad_sem)
                    if wait:
                        cp.wait()
                    else:
                        cp.start()

                off = off + jnp.where(has, bit, 0)
            return carry

        half_rows = MOE_ROWS // 2

        def fill_tail(h, carry, *, wait):
            cp = pltpu.make_async_copy(zrows, xs_out.at[pl.ds(pl.multiple_of(h * half_rows * ps, ps), half_rows * ps), :],
                                       pad_sem)
            if wait:
                cp.wait()
            else:
                cp.start()
            return carry

        tail = (2 * nvalid_ref[0], 2 * (xs_out.shape[0] // (MOE_ROWS * ps)))
        lax.fori_loop(0, MOE_EXPERTS, functools.partial(fill, wait=False), 0)
        lax.fori_loop(*tail, functools.partial(fill_tail, wait=False), 0)
        lax.fori_loop(0, MOE_EXPERTS, functools.partial(fill, wait=True), 0)
        lax.fori_loop(*tail, functools.partial(fill_tail, wait=True), 0)

    def start_rows(k):
        for tok in range(tm):
            row = dest_ref[0, 0, k * tm + tok]
            pltpu.make_async_copy(xp_ref.at[pl.ds(tok * ps, ps), :],
                                  xs_out.at[pl.ds(pl.multiple_of(row * ps, ps), ps), :],
                                  sem).start(priority=tok % DMA_PRIORITIES)

    xb = _unpack_rows(xp_ref, tm)
    for k in range(MOE_TOPK // 2):
        start_rows(k)
    gate = jnp.dot(xb, wsg_ref[...].astype(bf16), preferred_element_type=jnp.float32)
    up = jnp.dot(xb, wsu_ref[...].astype(bf16), preferred_element_type=jnp.float32)
    act = _silu(gate) * up
    for k in range(MOE_TOPK // 2, MOE_TOPK):
        start_rows(k)
    sh_ref[...] = jnp.dot(act.astype(bf16), wsd_ref[...].astype(bf16), preferred_element_type=jnp.float32)
    for _ in range(MOE_TOPK):
        pltpu.make_async_copy(xp_ref, xs_out.at[pl.ds(0, tm * ps), :], sem).wait()


def _dispatch_shared(x1p, dest_kt, pad_start, pad_len, nvalid, n_rows, ws_gate, ws_up, ws_down, layer, *, tm=512):
    k, t = dest_kt.shape
    ps, d = PACK_SUBLANES, D_MODEL
    nsteps = t // tm
    dest3 = dest_kt.reshape(k, nsteps, tm).transpose(1, 0, 2).reshape(nsteps, 1, k * tm)
    wspec = lambda shape: pl.BlockSpec((None,) + shape, lambda i, *_: (layer, 0, 0))
    grid_spec = pltpu.PrefetchScalarGridSpec(
        num_scalar_prefetch=3,
        grid=(nsteps,),
        in_specs=[pl.BlockSpec((1, 1, k * tm), lambda i, *_: (i, 0, 0), memory_space=pltpu.SMEM),
                  pl.BlockSpec((tm * ps, LANES), lambda i, *_: (i, 0)),
                  wspec((d, MOE_FFN)), wspec((d, MOE_FFN)), wspec((MOE_FFN, d))],
        out_specs=[pl.BlockSpec(memory_space=pl.ANY), pl.BlockSpec((tm, d), lambda i, *_: (i, 0))],
        scratch_shapes=[pltpu.VMEM((MOE_ROWS // 2 * ps, LANES), jnp.int32),
                        pltpu.SemaphoreType.DMA(()), pltpu.SemaphoreType.DMA(())],
    )
    return pl.pallas_call(
        functools.partial(_dispatch_kernel, tm=tm),
        grid_spec=grid_spec,
        out_shape=[jax.ShapeDtypeStruct((n_rows * ps, LANES), jnp.int32),
                   jax.ShapeDtypeStruct((t, d), jnp.float32)],
        compiler_params=_cparams("arbitrary"),
        name="dispatch_shared",
    )(pad_start, pad_len, nvalid, dest3, x1p, ws_gate, ws_up, ws_down)


def _expert_kernel(blk_e_ref, nvalid_ref, xs_ref, wg_ref, wu_ref, wd_ref, y_ref, wg_bf, wu_bf, wd_bf):
    i = pl.program_id(0)
    rows = MOE_ROWS
    bf16 = jnp.bfloat16

    @pl.when((i == 0) | (blk_e_ref[i] != blk_e_ref[jnp.maximum(i - 1, 0)]))
    def _():
        wg_bf[...] = wg_ref[...].astype(bf16)
        wu_bf[...] = wu_ref[...].astype(bf16)
        wd_bf[...] = wd_ref[...].astype(bf16)

    @pl.when(i < nvalid_ref[0])
    def _():
        xb = _unpack_rows(xs_ref, rows)
        gate = jnp.dot(xb, wg_bf[...], preferred_element_type=jnp.float32)
        up = jnp.dot(xb, wu_bf[...], preferred_element_type=jnp.float32)
        act = _silu(gate) * up
        y = jnp.dot(act.astype(bf16), wd_bf[...], preferred_element_type=jnp.float32)
        _pack_rows(y, y_ref, rows)

    @pl.when(i >= nvalid_ref[0])
    def _():
        y_ref[...] = jnp.zeros_like(y_ref)


def _routed_experts(xs, blk_expert, nvalid, w_gate, w_up, w_down, layer):
    nb = blk_expert.shape[0]
    rows, d = MOE_ROWS, D_MODEL
    wspec = lambda shape: pl.BlockSpec((None, None) + shape, lambda i, be, nv: (layer, be[i], 0, 0))
    grid_spec = pltpu.PrefetchScalarGridSpec(
        num_scalar_prefetch=2,
        grid=(nb,),
        in_specs=[pl.BlockSpec((rows * PACK_SUBLANES, LANES), lambda i, be, nv: (jnp.minimum(i, nv[0] - 1), 0)),
                  wspec((d, MOE_FFN)), wspec((d, MOE_FFN)), wspec((MOE_FFN, d))],
        out_specs=pl.BlockSpec((rows * PACK_SUBLANES, LANES), lambda i, be, nv: (i, 0)),
        scratch_shapes=[pltpu.VMEM((d, MOE_FFN), jnp.bfloat16), pltpu.VMEM((d, MOE_FFN), jnp.bfloat16),
                        pltpu.VMEM((MOE_FFN, d), jnp.bfloat16)],
    )
    return pl.pallas_call(
        _expert_kernel,
        grid_spec=grid_spec,
        out_shape=jax.ShapeDtypeStruct((nb * rows * PACK_SUBLANES, LANES), jnp.int32),
        compiler_params=_cparams("arbitrary"),
        name="routed_experts",
    )(blk_expert, nvalid, xs, w_gate, w_up, w_down)


def _combine_kernel(dest_ref, dest_next_ref, ys_hbm, w_ref, x_ref, sh_ref, g_ref, b_ref, o_ref, ob_ref,
                    ybuf0, ybuf1, acc, sem, *, alpha, nsteps):
    i = pl.program_id(0)
    tc, ps = COMBINE_TOK, PACK_SUBLANES
    n_rows = tc * MOE_TOPK
    half = D_MODEL // 2
    f32 = jnp.float32
    bufs = (ybuf0, ybuf1)

    def row_copy(d_ref, r, slot):
        row = d_ref[0, 0, r]
        return pltpu.make_async_copy(ys_hbm.at[pl.ds(pl.multiple_of(row * ps, ps), ps), :],
                                     bufs[slot].at[pl.ds(pl.multiple_of(r * ps, ps), ps), :], sem.at[slot])

    def wait_block(slot):
        pltpu.make_async_copy(ys_hbm.at[pl.ds(0, n_rows * ps), :], bufs[slot], sem.at[slot]).wait()

    @pl.when(i == 0)
    def _():
        def body(pair, carry):
            for prio in range(DMA_PRIORITIES):
                row_copy(dest_ref, pair * DMA_PRIORITIES + prio, 0).start(priority=prio)
            return carry
        lax.fori_loop(0, n_rows // DMA_PRIORITIES, body, 0, unroll=4)

    rows_per_chunk = n_rows // ps
    for slot in range(2):
        @pl.when(lax.rem(i, 2) == slot)
        def _(slot=slot):
            wait_block(slot)
            for c in range(ps):
                for r in range(c * rows_per_chunk, (c + 1) * rows_per_chunk):
                    row_copy(dest_next_ref, r, 1 - slot).start(priority=r % DMA_PRIORITIES)
                lo_cols = slice(c * LANES, (c + 1) * LANES)
                hi_cols = slice(half + c * LANES, half + (c + 1) * LANES)
                a_lo = alpha * x_ref[:, lo_cols] + sh_ref[:, lo_cols]
                a_hi = alpha * x_ref[:, hi_cols] + sh_ref[:, hi_cols]
                for k in range(MOE_TOPK):
                    word = bufs[slot][pl.ds(k * tc * ps + c, tc, stride=ps), :]
                    wk = w_ref[:, k:k + 1]
                    a_lo = a_lo + wk * lax.bitcast_convert_type(lax.shift_left(word, 16), f32)
                    a_hi = a_hi + wk * lax.bitcast_convert_type(word & HIGH_HALF_MASK, f32)
                acc[:, lo_cols] = a_lo
                acc[:, hi_cols] = a_hi

            @pl.when(i == nsteps - 1)
            def _():
                wait_block(1 - slot)

    x2 = _layer_norm_rows(acc[...], g_ref[...], b_ref[...])
    o_ref[...] = x2
    ob_ref[...] = x2.astype(ob_ref.dtype)


def _combine_ln(ys, dest_kt, wts_tk, x1, shared, g, b, *, alpha):
    t, d = x1.shape
    tc = COMBINE_TOK
    nsteps = t // tc
    dest3 = dest_kt.reshape(MOE_TOPK, nsteps, tc).transpose(1, 0, 2).reshape(nsteps, 1, MOE_TOPK * tc)
    smem_blk = lambda f: pl.BlockSpec((1, 1, MOE_TOPK * tc), f, memory_space=pltpu.SMEM)
    row = lambda i: (i, 0)
    const = lambda i: (0, 0)
    return pl.pallas_call(
        functools.partial(_combine_kernel, alpha=alpha, nsteps=nsteps),
        grid=(nsteps,),
        in_specs=[smem_blk(lambda i: (i, 0, 0)),
                  smem_blk(lambda i: (jnp.minimum(i + 1, nsteps - 1), 0, 0)),
                  pl.BlockSpec(memory_space=pl.ANY),
                  pl.BlockSpec((tc, MOE_TOPK), row),
                  pl.BlockSpec((tc, d), row), pl.BlockSpec((tc, d), row),
                  pl.BlockSpec((1, d), const), pl.BlockSpec((1, d), const)],
        out_specs=[pl.BlockSpec((tc, d), row), pl.BlockSpec((tc, d), row)],
        out_shape=[jax.ShapeDtypeStruct((t, d), jnp.float32), jax.ShapeDtypeStruct((t, d), jnp.bfloat16)],
        scratch_shapes=[pltpu.VMEM((MOE_TOPK * tc * PACK_SUBLANES, LANES), jnp.int32),
                        pltpu.VMEM((MOE_TOPK * tc * PACK_SUBLANES, LANES), jnp.int32),
                        pltpu.VMEM((tc, d), jnp.float32),
                        pltpu.SemaphoreType.DMA((2,))],
        compiler_params=_cparams("arbitrary"),
        name="combine_ln",
    )(dest3, dest3, ys, wts_tk, x1, shared, g.reshape(1, d), b.reshape(1, d))


def _dispatch_plan(idx_kt, rank_kt, counts):
    k, t = idx_kt.shape
    e, rows = MOE_EXPERTS, MOE_ROWS
    nb = k * t // rows + e
    padded = (counts + rows - 1) // rows * rows
    pend = jnp.cumsum(padded)
    pstart = pend - padded
    blk_start = jnp.arange(nb, dtype=jnp.int32) * rows
    blk_expert = jnp.minimum(jnp.sum(pend[None, :] <= blk_start[:, None], axis=1), e - 1).astype(jnp.int32)
    nvalid = (pend[-1] // rows).astype(jnp.int32).reshape(1)
    onehot = idx_kt[:, :, None] == jnp.arange(e, dtype=jnp.int32)
    dest = rank_kt + jnp.sum(jnp.where(onehot, pstart, 0), axis=-1).astype(jnp.int32)
    pad_start = (pstart + counts).astype(jnp.int32)
    pad_len = (padded - counts).astype(jnp.int32)
    return blk_expert, nvalid, dest, pad_start, pad_len, nb


def kernel(x, w_in, conv_a_w, conv_a_b, ln_a_g, ln_a_b, conv_c_w, w_branch, w_out, ln1_g, ln1_b, router_w,
           router_bias, w_gate, w_up, w_down, ws_gate, ws_up, ws_down, ln2_g, ln2_b):
    b, s, d = x.shape
    depth = w_in.shape[0]
    t = b * s
    alpha = (DEPTH_ALPHA_BASE * depth) ** 0.25
    bf16 = jnp.bfloat16
    xf = x.reshape(t, d)
    xb = xf.astype(bf16)
    for l in range(depth):
        u = _in_proj(xb, w_in, l, tm=min(t, 1024), tn=1024)
        u3 = u.reshape(b, s, N_IN)
        a, c = _conv_branches(u3, conv_a_w[l], conv_a_b[l], ln_a_g[l], ln_a_b[l], conv_c_w[l])
        att = _attention(u3)
        merged = _branch_merge(a.reshape(t, BRANCH_W), att.reshape(t, BRANCH_W), c.reshape(t, BRANCH_W), u,
                               w_branch[l].astype(bf16))
        x1, x1p = _outproj_ln(merged, w_out[l].astype(bf16), xf, ln1_g[l], ln1_b[l], alpha=alpha)
        idx_kt, wts_kt, rank_kt, counts = _router(x1, router_w[l], router_bias[l])
        blk_expert, nvalid, dest, pad_start, pad_len, nb = _dispatch_plan(idx_kt, rank_kt,
                                                                          counts[:, 0].astype(jnp.int32))
        xs, shared = _dispatch_shared(x1p, dest, pad_start, pad_len, nvalid, nb * MOE_ROWS,
                                      ws_gate, ws_up, ws_down, l)
        ys = _routed_experts(xs, blk_expert, nvalid, w_gate, w_up, w_down, l)
        xf, xb = _combine_ln(ys, dest, wts_kt.T, x1, shared, ln2_g[l], ln2_b[l], alpha=alpha)
    return xf.reshape(b, s, d)
```
